```python
import math
import jax, jax.numpy as jnp
from jax import lax
import numpy as np

D_MODEL = 1024
BATCH = 16
SEQ = 4096
DEPTH = 1
DEC_BATCH = 32
DEC_SEQ = 2048
PAST_LEN = 128

ATT_HEADS = 16
ATT_KV_HEADS = 4
ATT_HEAD_DIM = 64
ATT_GROUP = ATT_HEADS // ATT_KV_HEADS
WINDOW = 128
ATT_BLOCK = 128
ATT_Q = ATT_HEADS * ATT_HEAD_DIM
ATT_KV = ATT_KV_HEADS * ATT_HEAD_DIM

HGRN_HEADS = 8
HGRN_DK = 128
HGRN_DV = 128
HGRN_CHUNK = 32
HG_K = HGRN_HEADS * HGRN_DK
HG_V = HGRN_HEADS * HGRN_DV

IN_SIZES = (ATT_Q, ATT_KV, ATT_KV, HG_K, HG_K, HG_K, HG_V, HG_V, D_MODEL, D_MODEL)
IN_COLS = ATT_Q + 2 * ATT_KV + 3 * HG_K + 2 * HG_V + 2 * D_MODEL

PEER_HEADS = 8
PEER_NKEYS = 128
PEER_EXPERTS = PEER_NKEYS * PEER_NKEYS
PEER_QDIM = 256
PEER_HALF = PEER_QDIM // 2
PEER_TOPK = 16
PEER_CHUNK = 128

EPS = 1e-6

kernel_name = "hybrid_swa_hgrn2_peer_adaln_encoder"


def rmsnorm(x, g):
    xf = x.astype(jnp.float32)
    y = xf * lax.rsqrt(jnp.mean(xf * xf, axis=-1, keepdims=True) + EPS)
    return (y * g.astype(jnp.float32)).astype(x.dtype)


def alibi_slopes():
    h = jnp.arange(1, ATT_HEADS + 1, dtype=jnp.float32)
    return jnp.exp2(-8.0 * h / ATT_HEADS)


def windowed_attention(q, k, v, sink):
    B, S = q.shape[0], q.shape[1]
    nblk = S // ATT_BLOCK
    span = ATT_BLOCK + 2 * WINDOW
    kp = jnp.pad(k, ((0, 0), (WINDOW, WINDOW), (0, 0), (0, 0)))
    vp = jnp.pad(v, ((0, 0), (WINDOW, WINDOW), (0, 0), (0, 0)))
    slopes = alibi_slopes().reshape(ATT_KV_HEADS, ATT_GROUP)
    sinkg = sink.astype(jnp.float32).reshape(ATT_KV_HEADS, ATT_GROUP)[None, :, :, None]
    scale = 1.0 / math.sqrt(ATT_HEAD_DIM)

    def block(j):
        start = j * ATT_BLOCK
        qb = lax.dynamic_slice_in_dim(q, start, ATT_BLOCK, axis=1)
        kb = lax.dynamic_slice_in_dim(kp, start, span, axis=1)
        vb = lax.dynamic_slice_in_dim(vp, start, span, axis=1)
        qpos = start + jnp.arange(ATT_BLOCK)
        kpos = start - WINDOW + jnp.arange(span)
        dist = jnp.abs(qpos[:, None] - kpos[None, :])
        valid = (dist <= WINDOW) & (kpos >= 0)[None, :] & (kpos < S)[None, :]
        s = jnp.einsum('bqhgd,bkhd->bhgqk', qb, kb, preferred_element_type=jnp.float32) * scale
        s = s - slopes[:, :, None, None] * dist.astype(jnp.float32)
        s = jnp.where(valid, s, -jnp.inf)
        m = jnp.maximum(jnp.max(s, axis=-1), sinkg)
        p = jnp.exp(s - m[..., None])
        denom = jnp.sum(p, axis=-1) + jnp.exp(sinkg - m)
        o = jnp.einsum('bhgqk,bkhd->bqhgd', p, vb.astype(jnp.float32))
        o = o / jnp.transpose(denom, (0, 3, 1, 2))[..., None]
        return o.astype(q.dtype)

    out = lax.map(block, jnp.arange(nblk))
    return jnp.moveaxis(out, 0, 1).reshape(B, S, ATT_Q)


def hgrn2_scan(q, logf, kk, v):
    B, S, H, _ = q.shape
    C = HGRN_CHUNK
    nC = S // C

    def to_chunks(a):
        return jnp.transpose(a.reshape(B, nC, C, H, a.shape[-1]), (1, 0, 3, 2, 4))

    mask = jnp.tril(jnp.ones((C, C), dtype=bool))[None, None, :, :, None]

    def step(state, inp):
        qc, lfc, kc, vc = inp
        b = jnp.cumsum(lfc, axis=2)
        o_inter = jnp.einsum('bhtk,bhkv->bhtv', qc * jnp.exp(b), state)
        diff = jnp.where(mask, b[:, :, :, None, :] - b[:, :, None, :, :], -jnp.inf)
        A = jnp.einsum('bhtk,bhtsk,bhsk->bhts', qc, jnp.exp(diff), kc)
        o = o_inter + jnp.einsum('bhts,bhsv->bhtv', A, vc)
        b_last = b[:, :, -1:, :]
        new_state = jnp.exp(b_last[:, :, 0, :])[..., None] * state + jnp.einsum(
            'bhsk,bhsv->bhkv', kc * jnp.exp(b_last - b), vc)
        return new_state, o

    s0 = jnp.zeros((B, H, q.shape[-1], v.shape[-1]), jnp.float32)
    _, o = lax.scan(step, s0, (to_chunks(q), to_chunks(logf), to_chunks(kk), to_chunks(v)))
    return jnp.transpose(o, (1, 0, 3, 2, 4)).reshape(B, S, H, v.shape[-1])


def lower_bound(p_lb, layer):
    lb = jnp.cumsum(jax.nn.softmax(p_lb.astype(jnp.float32), axis=0), axis=0)[layer]
    return lb.reshape(HGRN_HEADS, HGRN_DK)


def hgrn2_direction(qh, z, ih, lb, reverse):
    zf = z.astype(jnp.float32)
    f = lb + (1.0 - lb) * jax.nn.sigmoid(zf)
    kk = (1.0 - lb) * jax.nn.sigmoid(-zf)
    logf = jnp.log(f)
    if reverse:
        o = hgrn2_scan(jnp.flip(qh, 1), jnp.flip(logf, 1), jnp.flip(kk, 1), jnp.flip(ih, 1))
        return jnp.flip(o, 1)
    return hgrn2_scan(qh, logf, kk, ih)


def peer(h, w_pq, keys, u, v):
    B, S, D = h.shape
    hc = h.reshape(-1, PEER_CHUNK, D)
    T = PEER_CHUNK

    def chunk(xc):
        q = (xc @ w_pq).reshape(T, PEER_HEADS, 2, PEER_HALF)
        s = jnp.einsum('thpd,hpnd->thpn', q, keys, preferred_element_type=jnp.float32)
        sv, si = lax.top_k(s, PEER_TOPK)
        cand = sv[:, :, 0, :, None] + sv[:, :, 1, None, :]
        cand_id = si[:, :, 0, :, None] * PEER_NKEYS + si[:, :, 1, None, :]
        top_s, pos = lax.top_k(cand.reshape(T, PEER_HEADS, PEER_TOPK * PEER_TOPK), PEER_TOPK)
        eid = jnp.take_along_axis(cand_id.reshape(T, PEER_HEADS, PEER_TOPK * PEER_TOPK), pos, axis=-1)
        gate = jax.nn.softmax(top_s, axis=-1)
        ue = u[eid]
        act = jax.nn.gelu(jnp.einsum('thkd,td->thk', ue, xc, preferred_element_type=jnp.float32),
                          approximate=False)
        ve = v[eid]
        return jnp.einsum('thk,thkd->td', (gate * act).astype(xc.dtype), ve)

    return lax.map(chunk, hc).reshape(B, S, D)


def encoder_layer(x, c, layer, w_mod, b_mod, g_norm1, w_in, att_sink, p_lb_fwd, p_lb_bwd,
                  g_hnorm, w_o_att, w_o_hgrn, w_out, g_norm2, w_pq, peer_keys, peer_u, peer_v):
    B, S, D = x.shape
    mod = jax.nn.silu(c) @ w_mod + b_mod
    shift1, scale1, gate1, shift2, scale2, gate2 = [m[:, None, :] for m in jnp.split(mod, 6, axis=-1)]

    h = rmsnorm(x, g_norm1) * (1.0 + scale1) + shift1
    proj = h @ w_in
    cuts = np.cumsum(IN_SIZES)[:-1].tolist()
    qa, ka, va, qh, zf, zb, ih, gh, ga, gb = jnp.split(proj, cuts, axis=-1)

    att = windowed_attention(qa.reshape(B, S, ATT_KV_HEADS, ATT_GROUP, ATT_HEAD_DIM),
                             ka.reshape(B, S, ATT_KV_HEADS, ATT_HEAD_DIM),
                             va.reshape(B, S, ATT_KV_HEADS, ATT_HEAD_DIM), att_sink)
    y_att = att @ w_o_att

    qf = jax.nn.silu(qh.astype(jnp.float32)).reshape(B, S, HGRN_HEADS, HGRN_DK)
    iv = ih.astype(jnp.float32).reshape(B, S, HGRN_HEADS, HGRN_DV)
    o = (hgrn2_direction(qf, zf.reshape(B, S, HGRN_HEADS, HGRN_DK), iv, lower_bound(p_lb_fwd, layer), False)
         + hgrn2_direction(qf, zb.reshape(B, S, HGRN_HEADS, HGRN_DK), iv, lower_bound(p_lb_bwd, layer), True))
    o = o * lax.rsqrt(jnp.mean(o * o, axis=-1, keepdims=True) + EPS)
    o = o * g_hnorm.astype(jnp.float32).reshape(HGRN_HEADS, HGRN_DV)
    o = o * jax.nn.silu(gh.astype(jnp.float32)).reshape(B, S, HGRN_HEADS, HGRN_DV)
    y_hg = o.reshape(B, S, HG_V).astype(x.dtype) @ w_o_hgrn

    merged = jax.nn.sigmoid(ga) * y_att + jax.nn.sigmoid(gb) * y_hg
    x = x + gate1 * (merged @ w_out)

    h2 = rmsnorm(x, g_norm2) * (1.0 + scale2) + shift2
    x = x + gate2 * peer(h2, w_pq, peer_keys, peer_u, peer_v)
    return x


def setup_inputs(seed: int = 0) -> dict:
    key = jax.random.key(seed)
    ks = jax.random.split(key, 21)
    f32 = jnp.float32
    nrm = lambda k, shape, s: jax.random.normal(k, shape, f32) * s
    D = D_MODEL
    return {
        "x_prompt": nrm(ks[0], (BATCH, SEQ, D), 1.0),
        "x_sample": nrm(ks[1], (DEC_BATCH, DEC_SEQ, D), 1.0),
        "c_prompt": nrm(ks[2], (BATCH, D), 1.0),
        "c_sample": nrm(ks[3], (DEC_BATCH, D), 1.0),
        "w_mod": nrm(ks[4], (DEPTH, D, 6 * D), 0.5 * D ** -0.5),
        "b_mod": nrm(ks[5], (DEPTH, 6 * D), 0.02),
        "g_norm1": 1.0 + nrm(ks[6], (DEPTH, D), 0.02),
        "w_in": nrm(ks[7], (DEPTH, D, IN_COLS), D ** -0.5),
        "att_sink": nrm(ks[8], (DEPTH, ATT_HEADS), 0.5),
        "p_lb_fwd": nrm(ks[9], (DEPTH + 1, HG_K), 0.1),
        "p_lb_bwd": nrm(ks[10], (DEPTH + 1, HG_K), 0.1),
        "g_hnorm": 1.0 + nrm(ks[11], (DEPTH, HG_V), 0.02),
        "w_o_att": nrm(ks[12], (DEPTH, ATT_Q, D), ATT_Q ** -0.5),
        "w_o_hgrn": nrm(ks[13], (DEPTH, HG_V, D), HG_V ** -0.5),
        "w_out": nrm(ks[14], (DEPTH, D, D), D ** -0.5),
        "g_norm2": 1.0 + nrm(ks[15], (DEPTH, D), 0.02),
        "w_pq": nrm(ks[16], (DEPTH, D, PEER_HEADS * PEER_QDIM), D ** -0.5),
        "peer_keys": nrm(ks[17], (DEPTH, PEER_HEADS, 2, PEER_NKEYS, PEER_HALF), PEER_HALF ** -0.5),
        "peer_u": nrm(ks[18], (DEPTH, PEER_EXPERTS, D), D ** -0.5),
        "peer_v": nrm(ks[19], (DEPTH, PEER_EXPERTS, D), PEER_HEADS ** -0.5),
        "g_final": 1.0 + nrm(ks[20], (D,), 0.02),
    }


def reference(x_prompt, x_sample, c_prompt, c_sample, w_mod, b_mod, g_norm1, w_in, att_sink,
              p_lb_fwd, p_lb_bwd, g_hnorm, w_o_att, w_o_hgrn, w_out, g_norm2, w_pq, peer_keys,
              peer_u, peer_v, g_final):
    def run(x, c):
        for l in range(DEPTH):
            x = encoder_layer(x, c, l, w_mod[l], b_mod[l], g_norm1[l], w_in[l], att_sink[l],
                              p_lb_fwd, p_lb_bwd, g_hnorm[l], w_o_att[l], w_o_hgrn[l], w_out[l],
                              g_norm2[l], w_pq[l], peer_keys[l], peer_u[l], peer_v[l])
        return rmsnorm(x, g_final)

    y_prompt = run(x_prompt, c_prompt)
    y_sample = run(x_sample, c_sample)
    return (y_prompt, y_sample)
```

```python
import functools
import math

import jax
import jax.numpy as jnp
from jax import lax
from jax.experimental import pallas as pl
from jax.experimental.pallas import tpu as pltpu

F32 = jnp.float32
BF16 = jnp.bfloat16
HIGHEST = lax.Precision.HIGHEST

D_MODEL = 1024
ATT_HEADS = 16
ATT_KV_HEADS = 4
ATT_HEAD_DIM = 64
ATT_GROUP = ATT_HEADS // ATT_KV_HEADS
WINDOW = 128
ATT_Q = ATT_HEADS * ATT_HEAD_DIM
ATT_KV = ATT_KV_HEADS * ATT_HEAD_DIM
HGRN_HEADS = 8
HGRN_DK = 128
HG_K = HGRN_HEADS * HGRN_DK
HG_V = HG_K
IN_COLS = ATT_Q + 2 * ATT_KV + 3 * HG_K + 2 * HG_V + 2 * D_MODEL
PEER_HEADS = 8
PEER_NKEYS = 128
PEER_HALF = 128
PEER_TOPK = 16
PEER_SEL = PEER_HEADS * PEER_TOPK
EPS = 1e-6

COL_QA = 0
COL_KA = COL_QA + ATT_Q
COL_VA = COL_KA + ATT_KV
COL_QH = COL_VA + ATT_KV
COL_ZF = COL_QH + HG_K
COL_ZB = COL_ZF + HG_K
COL_IH = COL_ZB + HG_K
COL_GH = COL_IH + HG_V
COL_GA = COL_GH + HG_V
COL_GB = COL_GA + D_MODEL

LANES = 128
SUBLANES = 8
VMEM_LIMIT = 56 * 1024 * 1024

HG_CHUNK = 128
HG_SUB = SUBLANES
HG_LEVELS = (64, 32, 16, 8)


def _cparams(sem):
    return pltpu.CompilerParams(dimension_semantics=sem, vmem_limit_bytes=VMEM_LIMIT)


def _nt_dot(a, b):
    return lax.dot_general(a, b, (((1,), (1,)), ((), ())), preferred_element_type=F32)


def _tn_dot(a, b):
    return lax.dot_general(a, b, (((0,), (0,)), ((), ())), preferred_element_type=F32)


def _mod_kernel(c_ref, w_ref, b_ref, o_ref):
    c = c_ref[...]
    sc = c * jax.nn.sigmoid(c)
    o_ref[...] = jnp.dot(sc, w_ref[...], preferred_element_type=F32, precision=HIGHEST) + b_ref[...]


def _modulation(c, w_mod, b_mod):
    B = c.shape[0]
    n = w_mod.shape[1]
    tn = D_MODEL
    return pl.pallas_call(
        _mod_kernel,
        grid=(n // tn,),
        in_specs=[
            pl.BlockSpec((B, D_MODEL), lambda j: (0, 0)),
            pl.BlockSpec((D_MODEL, tn), lambda j: (0, j)),
            pl.BlockSpec((1, tn), lambda j: (0, j)),
        ],
        out_specs=pl.BlockSpec((B, tn), lambda j: (0, j)),
        out_shape=jax.ShapeDtypeStruct((B, n), F32),
        compiler_params=_cparams(("parallel",)),
    )(c, w_mod, b_mod.reshape(1, n))


def _inproj_kernel(x_ref, mod_ref, g_ref, w_ref, o_ref, h_scr):
    @pl.when(pl.program_id(2) == 0)
    def _():
        x = x_ref[...]
        y = x * lax.rsqrt(jnp.mean(x * x, axis=-1, keepdims=True) + EPS) * g_ref[...]
        h = y * (1.0 + mod_ref[1:2, :]) + mod_ref[0:1, :]
        h_scr[...] = h.astype(BF16)

    o_ref[...] = jnp.dot(h_scr[...], w_ref[...], preferred_element_type=F32)


def _inproj(x, mod3, g1, w_in_bf):
    B, S, _ = x.shape
    tm = 512
    tn = 2176
    return pl.pallas_call(
        _inproj_kernel,
        grid=(B, S // tm, IN_COLS // tn),
        in_specs=[
            pl.BlockSpec((None, tm, D_MODEL), lambda b, i, j: (b, i, 0)),
            pl.BlockSpec((None, 6, D_MODEL), lambda b, i, j: (b, 0, 0)),
            pl.BlockSpec((1, D_MODEL), lambda b, i, j: (0, 0)),
            pl.BlockSpec((D_MODEL, tn), lambda b, i, j: (0, j)),
        ],
        out_specs=pl.BlockSpec((None, tm, tn), lambda b, i, j: (b, i, j)),
        out_shape=jax.ShapeDtypeStruct((B, S, IN_COLS), F32),
        scratch_shapes=[pltpu.VMEM((tm, D_MODEL), BF16)],
        compiler_params=_cparams(("parallel", "parallel", "arbitrary")),
    )(x, mod3, g1, w_in_bf)


def _attn_kernel(sink_ref, q_ref, kp_ref, kc_ref, kn_ref, vp_ref, vc_ref, vn_ref, o_ref, *, nblk):
    j = pl.program_id(1)
    blk = WINDOW
    span = 3 * blk
    q = q_ref[...].astype(BF16)
    k = jnp.concatenate([kp_ref[...], kc_ref[...], kn_ref[...]], axis=0).astype(BF16)
    v = jnp.concatenate([vp_ref[...], vc_ref[...], vn_ref[...]], axis=0).astype(BF16)
    qi = lax.broadcasted_iota(jnp.int32, (blk, span), 0)
    kj = lax.broadcasted_iota(jnp.int32, (blk, span), 1)
    dist = jnp.abs(kj - blk - qi)
    lo = jnp.where(j == 0, blk, 0)
    hi = jnp.where(j == nblk - 1, 2 * blk, span)
    valid = (dist <= WINDOW) & (kj >= lo) & (kj < hi)
    distf = dist.astype(F32)
    scale = 1.0 / math.sqrt(ATT_HEAD_DIM)
    outs = []
    for hh in range(ATT_HEADS):
        h = hh // ATT_GROUP
        slope = 2.0 ** (-8.0 * (hh + 1) / ATT_HEADS)
        qh = q[:, hh * ATT_HEAD_DIM:(hh + 1) * ATT_HEAD_DIM]
        kh = k[:, h * ATT_HEAD_DIM:(h + 1) * ATT_HEAD_DIM]
        vh = v[:, h * ATT_HEAD_DIM:(h + 1) * ATT_HEAD_DIM]
        s = _nt_dot(qh, kh) * scale - slope * distf
        s = jnp.where(valid, s, -jnp.inf)
        sink = sink_ref[hh]
        m = jnp.maximum(jnp.max(s, axis=-1, keepdims=True), sink)
        p = jnp.exp(s - m)
        denom = jnp.sum(p, axis=-1, keepdims=True) + jnp.exp(sink - m)
        o = jnp.dot(p.astype(BF16), vh, preferred_element_type=F32) / denom
        outs.append(o)
    o_ref[...] = jnp.concatenate(outs, axis=-1).astype(o_ref.dtype)


def _attention(proj, sink):
    B, S, _ = proj.shape
    blk = WINDOW
    nblk = S // blk
    kcol = COL_KA // ATT_KV
    vcol = COL_VA // ATT_KV

    def kv_spec(col, off):
        return pl.BlockSpec(
            (None, blk, ATT_KV),
            lambda b, j: (b, jnp.clip(j + off, 0, nblk - 1), col))

    return pl.pallas_call(
        functools.partial(_attn_kernel, nblk=nblk),
        grid=(B, nblk),
        in_specs=[
            pl.BlockSpec(memory_space=pltpu.SMEM),
            pl.BlockSpec((None, blk, ATT_Q), lambda b, j: (b, j, 0)),
            kv_spec(kcol, -1), kv_spec(kcol, 0), kv_spec(kcol, 1),
            kv_spec(vcol, -1), kv_spec(vcol, 0), kv_spec(vcol, 1),
        ],
        out_specs=pl.BlockSpec((None, blk, ATT_Q), lambda b, j: (b, j, 0)),
        out_shape=jax.ShapeDtypeStruct((B, S, ATT_Q), BF16),
        compiler_params=_cparams(("parallel", "parallel")),
    )(sink, proj, proj, proj, proj, proj, proj, proj)


def _hgrn_chunk(q_raw, z, v, lb, st_scr, b_scr, q_scr, k_scr, v_scr, od_scr, reverse):
    C = HG_CHUNK
    q = q_raw * jax.nn.sigmoid(q_raw)
    f = lb + (1.0 - lb) * jax.nn.sigmoid(z)
    kk = (1.0 - lb) * jax.nn.sigmoid(-z)
    logf = jnp.log(f)
    row = lax.broadcasted_iota(jnp.int32, (C, C), 0)
    col = lax.broadcasted_iota(jnp.int32, (C, C), 1)
    tri = jnp.where((row <= col) if reverse else (row >= col), 1.0, 0.0).astype(F32)
    b = jnp.dot(tri, logf, preferred_element_type=F32, precision=HIGHEST)
    edge = b[0:1, :] if reverse else b[C - 1:C, :]
    vb = v.astype(BF16)

    st = st_scr[...]
    o = _nt_dot((q * jnp.exp(b)).astype(BF16), st.astype(BF16))
    kh = (kk * jnp.exp(edge - b)).astype(BF16)
    st_scr[...] = st * jnp.exp(edge) + _tn_dot(vb, kh)

    pos = lax.broadcasted_iota(jnp.int32, (C, 1), 0)
    a = jnp.zeros((C, C), F32)
    for hsz in HG_LEVELS:
        seg = 2 * hsz
        refs = []
        for s0 in range(0, C, seg):
            r = s0 + hsz if reverse else s0 + hsz - 1
            refs.append(jnp.broadcast_to(b[r:r + 1, :], (seg, HGRN_DK)))
        bmid = jnp.concatenate(refs, axis=0) if len(refs) > 1 else refs[0]
        e = jnp.exp(-jnp.abs(b - bmid))
        upper = (pos % seg) >= hsz
        qrole = jnp.logical_not(upper) if reverse else upper
        ql = jnp.where(qrole, q * e, 0.0).astype(BF16)
        kl = jnp.where(qrole, 0.0, kk * e).astype(BF16)
        al = _nt_dot(ql, kl)
        same = (row // seg) == (col // seg)
        rq = (row % seg) >= hsz
        ck = (col % seg) >= hsz
        if reverse:
            m = same & jnp.logical_not(rq) & ck
        else:
            m = same & rq & jnp.logical_not(ck)
        a = jnp.where(m, al, a)
    o = o + jnp.dot(a.astype(BF16), vb, preferred_element_type=F32)

    b_scr[...] = b
    q_scr[...] = q
    k_scr[...] = kk
    v_scr[...] = v
    sub = lax.broadcasted_iota(jnp.int32, (HG_SUB, HGRN_DK), 0)

    def diag(i, carry):
        r0 = pl.multiple_of(i * HG_SUB, HG_SUB)
        b8 = b_scr[pl.ds(r0, HG_SUB), :]
        q8 = q_scr[pl.ds(r0, HG_SUB), :]
        k8 = k_scr[pl.ds(r0, HG_SUB), :]
        v8 = v_scr[pl.ds(r0, HG_SUB), :]
        acc = jnp.zeros((HG_SUB, HGRN_DK), F32)
        for t in range(HG_SUB):
            arg = jnp.minimum(b8[t:t + 1, :] - b8, 0.0)
            keep = (sub >= t) if reverse else (sub <= t)
            g = jnp.where(keep, q8[t:t + 1, :] * k8 * jnp.exp(arg), 0.0)
            w = jnp.sum(g, axis=-1, keepdims=True)
            r = jnp.sum(w * v8, axis=0, keepdims=True)
            acc = jnp.where(sub == t, r, acc)
        od_scr[pl.ds(r0, HG_SUB), :] = acc
        return carry

    lax.fori_loop(0, C // HG_SUB, diag, 0)
    return o + od_scr[...]


def _lower_bound(plb_ref):
    p = plb_ref[...]
    e = jnp.exp(p - jnp.max(p, axis=0, keepdims=True))
    return e[0:1, :] / jnp.sum(e, axis=0, keepdims=True)


def _hgrn_fwd_kernel(q_ref, z_ref, v_ref, plb_ref, o_ref, st_scr, b_scr, q_scr, k_scr, v_scr, od_scr,
                     *, nchunk):
    @pl.when(pl.program_id(2) == 0)
    def _():
        st_scr[...] = jnp.zeros_like(st_scr)

    lb = _lower_bound(plb_ref)
    C = HG_CHUNK
    for ci in range(nchunk):
        sl = slice(ci * C, (ci + 1) * C)
        o_ref[sl, :] = _hgrn_chunk(q_ref[sl, :], z_ref[sl, :], v_ref[sl, :], lb, st_scr, b_scr, q_scr,
                                   k_scr, v_scr, od_scr, False)


def _hgrn_bwd_kernel(q_ref, z_ref, v_ref, plb_ref, of_ref, gh_ref, gn_ref, o_ref, st_scr, b_scr, q_scr,
                     k_scr, v_scr, od_scr, *, nchunk):
    @pl.when(pl.program_id(2) == 0)
    def _():
        st_scr[...] = jnp.zeros_like(st_scr)

    lb = _lower_bound(plb_ref)
    C = HG_CHUNK
    for ci in reversed(range(nchunk)):
        sl = slice(ci * C, (ci + 1) * C)
        o = _hgrn_chunk(q_ref[sl, :], z_ref[sl, :], v_ref[sl, :], lb, st_scr, b_scr, q_scr, k_scr,
                        v_scr, od_scr, True)
        o = o + of_ref[sl, :]
        o = o * lax.rsqrt(jnp.mean(o * o, axis=-1, keepdims=True) + EPS) * gn_ref[...]
        gh = gh_ref[sl, :]
        o_ref[sl, :] = (o * (gh * jax.nn.sigmoid(gh))).astype(o_ref.dtype)


def _hgrn(proj, p_lb_fwd, p_lb_bwd, g_hnorm):
    B, S, _ = proj.shape
    tb = 512
    nt = S // tb
    nchunk = tb // HG_CHUNK
    dk = HGRN_DK
    scratch = [pltpu.VMEM((dk, dk), F32)] + [pltpu.VMEM((HG_CHUNK, dk), F32) for _ in range(5)]

    def col_spec(col0, rev):
        if rev:
            return pl.BlockSpec((None, tb, dk), lambda b, h, t: (b, nt - 1 - t, col0 // dk + h))
        return pl.BlockSpec((None, tb, dk), lambda b, h, t: (b, t, col0 // dk + h))

    plb_spec = pl.BlockSpec((p_lb_fwd.shape[0], dk), lambda b, h, t: (0, h))
    sem = ("parallel", "parallel", "arbitrary")
    o_f = pl.pallas_call(
        functools.partial(_hgrn_fwd_kernel, nchunk=nchunk),
        grid=(B, HGRN_HEADS, nt),
        in_specs=[col_spec(COL_QH, False), col_spec(COL_ZF, False), col_spec(COL_IH, False), plb_spec],
        out_specs=pl.BlockSpec((None, tb, dk), lambda b, h, t: (b, t, h)),
        out_shape=jax.ShapeDtypeStruct((B, S, HG_V), F32),
        scratch_shapes=scratch,
        compiler_params=_cparams(sem),
    )(proj, proj, proj, p_lb_fwd)
    return pl.pallas_call(
        functools.partial(_hgrn_bwd_kernel, nchunk=nchunk),
        grid=(B, HGRN_HEADS, nt),
        in_specs=[col_spec(COL_QH, True), col_spec(COL_ZB, True), col_spec(COL_IH, True), plb_spec,
                  pl.BlockSpec((None, tb, dk), lambda b, h, t: (b, nt - 1 - t, h)),
                  col_spec(COL_GH, True),
                  pl.BlockSpec((1, dk), lambda b, h, t: (0, h))],
        out_specs=pl.BlockSpec((None, tb, dk), lambda b, h, t: (b, nt - 1 - t, h)),
        out_shape=jax.ShapeDtypeStruct((B, S, HG_V), BF16),
        scratch_shapes=scratch,
        compiler_params=_cparams(sem),
    )(proj, proj, proj, p_lb_bwd, o_f, proj, g_hnorm)


def _outproj_kernel(att_ref, hg_ref, ga0_ref, ga1_ref, gb0_ref, gb1_ref, x_ref, mod_ref, woa_ref, woh_ref,
                    wout_ref, g2_ref, wpq_ref, x1_ref, h2_ref, qp_ref):
    ya = jnp.dot(att_ref[...], woa_ref[...], preferred_element_type=F32)
    yh = jnp.dot(hg_ref[...], woh_ref[...], preferred_element_type=F32)
    ga = jnp.concatenate([ga0_ref[...], ga1_ref[...]], axis=-1)
    gb = jnp.concatenate([gb0_ref[...], gb1_ref[...]], axis=-1)
    merged = jax.nn.sigmoid(ga) * ya + jax.nn.sigmoid(gb) * yh
    x1 = x_ref[...] + mod_ref[2:3, :] * jnp.dot(merged.astype(BF16), wout_ref[...],
                                                preferred_element_type=F32)
    x1_ref[...] = x1
    y = x1 * lax.rsqrt(jnp.mean(x1 * x1, axis=-1, keepdims=True) + EPS) * g2_ref[...]
    h2 = y * (1.0 + mod_ref[4:5, :]) + mod_ref[3:4, :]
    h2_ref[...] = h2
    qp_ref[...] = jnp.dot(h2.astype(BF16), wpq_ref[...], preferred_element_type=F32)


def _outproj(att, hg, proj, x, mod3, woa, woh, wout, g2, wpq):
    B, S, _ = x.shape
    tm = 256
    half = D_MODEL // 2
    nq = wpq.shape[1]

    def tok(width, col):
        return pl.BlockSpec((None, tm, width), lambda b, i: (b, i, col))

    def full(shape):
        return pl.BlockSpec(shape, lambda b, i: (0, 0))

    return pl.pallas_call(
        _outproj_kernel,
        grid=(B, S // tm),
        in_specs=[
            tok(ATT_Q, 0), tok(HG_V, 0),
            tok(half, COL_GA // half), tok(half, COL_GA // half + 1),
            tok(half, COL_GB // half), tok(half, COL_GB // half + 1),
            tok(D_MODEL, 0),
            pl.BlockSpec((None, 6, D_MODEL), lambda b, i: (b, 0, 0)),
            full((ATT_Q, D_MODEL)), full((HG_V, D_MODEL)), full((D_MODEL, D_MODEL)),
            full((1, D_MODEL)), full((D_MODEL, nq)),
        ],
        out_specs=[tok(D_MODEL, 0), tok(D_MODEL, 0), tok(nq, 0)],
        out_shape=[jax.ShapeDtypeStruct((B, S, D_MODEL), F32),
                   jax.ShapeDtypeStruct((B, S, D_MODEL), F32),
                   jax.ShapeDtypeStruct((B, S, nq), F32)],
        compiler_params=_cparams(("parallel", "parallel")),
    )(att, hg, proj, proj, proj, proj, x, mod3, woa, woh, wout, g2, wpq)


def _route_kernel(q_ref, keys_ref, eid_ref, gate_ref, sv_scr, si_scr, cand_scr, cid_scr, ts_scr):
    tt = q_ref.shape[0]
    n = PEER_NKEYS
    k = PEER_TOPK
    q = q_ref[...].astype(BF16)
    iota_n = lax.broadcasted_iota(jnp.int32, (n, tt), 0)
    for p in range(2):
        s = _nt_dot(keys_ref[p * n:(p + 1) * n, :].astype(BF16), q[:, p * PEER_HALF:(p + 1) * PEER_HALF])
        for i in range(k):
            m = jnp.max(s, axis=0, keepdims=True)
            idx = jnp.min(jnp.where(s == m, iota_n, n), axis=0, keepdims=True)
            sv_scr[p, i:i + 1, :] = m
            si_scr[p, i:i + 1, :] = idx
            s = jnp.where(iota_n == idx, -jnp.inf, s)
    sv1 = sv_scr[1]
    si1 = si_scr[1]
    for a in range(k):
        cand_scr[a * k:(a + 1) * k, :] = sv_scr[0, a:a + 1, :] + sv1
        cid_scr[a * k:(a + 1) * k, :] = si_scr[0, a:a + 1, :] * n + si1
    c = cand_scr[...]
    ids = cid_scr[...]
    iota_c = lax.broadcasted_iota(jnp.int32, (k * k, tt), 0)
    for i in range(k):
        m = jnp.max(c, axis=0, keepdims=True)
        pos = jnp.min(jnp.where(c == m, iota_c, k * k), axis=0, keepdims=True)
        sel = iota_c == pos
        eid_ref[i:i + 1, :] = jnp.max(jnp.where(sel, ids, -1), axis=0, keepdims=True)
        ts_scr[i:i + 1, :] = m
        c = jnp.where(sel, -jnp.inf, c)
    ts = ts_scr[...]
    ex = jnp.exp(ts - ts[0:1, :])
    gate_ref[...] = ex / jnp.sum(ex, axis=0, keepdims=True)


def _route(qp, keys2d):
    n_tok = qp.shape[0]
    tt = 256
    k = PEER_TOPK
    qd = 2 * PEER_HALF
    return pl.pallas_call(
        _route_kernel,
        grid=(n_tok // tt, PEER_HEADS),
        in_specs=[
            pl.BlockSpec((tt, qd), lambda i, h: (i, h)),
            pl.BlockSpec((2 * PEER_NKEYS, PEER_HALF), lambda i, h: (h, 0)),
        ],
        out_specs=[pl.BlockSpec((k, tt), lambda i, h: (h, i)),
                   pl.BlockSpec((k, tt), lambda i, h: (h, i))],
        out_shape=[jax.ShapeDtypeStruct((PEER_SEL, n_tok), jnp.int32),
                   jax.ShapeDtypeStruct((PEER_SEL, n_tok), F32)],
        scratch_shapes=[pltpu.VMEM((2, k, tt), F32), pltpu.VMEM((2, k, tt), jnp.int32),
                        pltpu.VMEM((k * k, tt), F32), pltpu.VMEM((k * k, tt), jnp.int32),
                        pltpu.VMEM((k, tt), F32)],
        compiler_params=_cparams(("parallel", "arbitrary")),
    )(qp, keys2d)


PEER_TT = 8
PEER_UNROLL = 8


def _peer_kernel(eid_ref, eidn_ref, gate_ref, h2_ref, x1_ref, mod_ref, gf_ref, u_hbm, v_hbm, o_ref,
                 ubuf, vbuf, usem, vsem):
    i = pl.program_id(0)
    n = pl.num_programs(0)
    tt = PEER_TT
    slot = i % 2

    def row_copies(idx_ref, t, j, s):
        e = idx_ref[t, j]
        return (pltpu.make_async_copy(u_hbm.at[pl.ds(e, 1)], ubuf.at[s, t, pl.ds(j, 1)], usem.at[s]),
                pltpu.make_async_copy(v_hbm.at[pl.ds(e, 1)], vbuf.at[s, t, pl.ds(j, 1)], vsem.at[s]))

    def issue(idx_ref, s):
        for t in range(tt):
            def body(g, carry):
                for r in range(PEER_UNROLL):
                    cu, cv = row_copies(idx_ref, t, g * PEER_UNROLL + r, s)
                    cu.start()
                    cv.start()
                return carry
            lax.fori_loop(0, PEER_SEL // PEER_UNROLL, body, 0)

    @pl.when(i == 0)
    def _():
        issue(eid_ref, 0)

    @pl.when(i + 1 < n)
    def _():
        issue(eidn_ref, 1 - slot)

    pltpu.make_async_copy(ubuf.at[slot], ubuf.at[slot], usem.at[slot]).wait()
    pltpu.make_async_copy(vbuf.at[slot], vbuf.at[slot], vsem.at[slot]).wait()

    gate_t = gate_ref[...].T
    gate2 = mod_ref[5:6, :]
    for t in range(tt):
        h2 = h2_ref[t:t + 1, :]
        act = jnp.sum(ubuf[slot, t] * h2, axis=-1, keepdims=True)
        act = 0.5 * act * (1.0 + lax.erf(act * (1.0 / math.sqrt(2.0))))
        w = gate_t[:, t:t + 1] * act
        y = jnp.sum(w * vbuf[slot, t], axis=0, keepdims=True)
        x2 = x1_ref[t:t + 1, :] + gate2 * y
        o_ref[t:t + 1, :] = x2 * lax.rsqrt(jnp.mean(x2 * x2, axis=-1, keepdims=True) + EPS) * gf_ref[...]


def _peer(eid_t, gate_t, h2, x1, mod3, g_final, u, v, seq):
    n_tok = h2.shape[0]
    tt = PEER_TT
    n = n_tok // tt
    smem_idx = lambda f: pl.BlockSpec((tt, PEER_SEL), f, memory_space=pltpu.SMEM)
    tok = lambda w: pl.BlockSpec((tt, w), lambda i: (i, 0))
    return pl.pallas_call(
        _peer_kernel,
        grid=(n,),
        in_specs=[
            smem_idx(lambda i: (i, 0)),
            smem_idx(lambda i: (jnp.minimum(i + 1, n - 1), 0)),
            tok(PEER_SEL), tok(D_MODEL), tok(D_MODEL),
            pl.BlockSpec((None, 6, D_MODEL), lambda i: ((i * tt) // seq, 0, 0)),
            pl.BlockSpec((1, D_MODEL), lambda i: (0, 0)),
            pl.BlockSpec(memory_space=pl.ANY),
            pl.BlockSpec(memory_space=pl.ANY),
        ],
        out_specs=tok(D_MODEL),
        out_shape=jax.ShapeDtypeStruct((n_tok, D_MODEL), F32),
        scratch_shapes=[pltpu.VMEM((2, tt, PEER_SEL, D_MODEL), F32),
                        pltpu.VMEM((2, tt, PEER_SEL, D_MODEL), F32),
                        pltpu.SemaphoreType.DMA((2,)), pltpu.SemaphoreType.DMA((2,))],
        compiler_params=_cparams(("arbitrary",)),
    )(eid_t, eid_t, gate_t, h2, x1, mod3, g_final, u, v)


def _run_group(x, c, w_mod, b_mod, g1, w_in_bf, sink, p_lb_fwd, p_lb_bwd, g_hnorm, woa, woh, wout, g2,
               wpq, keys2d, u, v, g_final):
    B, S, _ = x.shape
    mod3 = _modulation(c, w_mod, b_mod).reshape(B, 6, D_MODEL)
    proj = _inproj(x, mod3, g1, w_in_bf)
    att = _attention(proj, sink)
    hg = _hgrn(proj, p_lb_fwd, p_lb_bwd, g_hnorm)
    x1, h2, qp = _outproj(att, hg, proj, x, mod3, woa, woh, wout, g2, wpq)
    n_tok = B * S
    eid, gate = _route(qp.reshape(n_tok, -1), keys2d)
    y = _peer(eid.T, gate.T, h2.reshape(n_tok, D_MODEL), x1.reshape(n_tok, D_MODEL), mod3, g_final, u, v, S)
    return y.reshape(B, S, D_MODEL)


def kernel(x_prompt, x_sample, c_prompt, c_sample, w_mod, b_mod, g_norm1, w_in, att_sink, p_lb_fwd, p_lb_bwd,
           g_hnorm, w_o_att, w_o_hgrn, w_out, g_norm2, w_pq, peer_keys, peer_u, peer_v, g_final):
    assert w_mod.shape[0] == 1 and p_lb_fwd.shape[0] == 2, "single-layer encoder"
    shared = (
        w_mod[0], b_mod[0], g_norm1[0].reshape(1, D_MODEL), w_in[0].astype(BF16), att_sink[0],
        p_lb_fwd, p_lb_bwd, g_hnorm[0].reshape(1, HG_V),
        w_o_att[0].astype(BF16), w_o_hgrn[0].astype(BF16), w_out[0].astype(BF16),
        g_norm2[0].reshape(1, D_MODEL), w_pq[0].astype(BF16),
        peer_keys[0].reshape(PEER_HEADS * 2 * PEER_NKEYS, PEER_HALF),
        peer_u[0], peer_v[0], g_final.reshape(1, D_MODEL),
    )
    return (_run_group(x_prompt, c_prompt, *shared), _run_group(x_sample, c_sample, *shared))
```

```python
import functools
import math

import jax
import jax.numpy as jnp
from jax import lax
from jax.experimental import pallas as pl
from jax.experimental.pallas import tpu as pltpu

F32 = jnp.float32
BF16 = jnp.bfloat16
HIGHEST = lax.Precision.HIGHEST

D_MODEL = 1024
ATT_HEADS = 16
ATT_KV_HEADS = 4
ATT_HEAD_DIM = 64
ATT_GROUP = ATT_HEADS // ATT_KV_HEADS
WINDOW = 128
ATT_Q = ATT_HEADS * ATT_HEAD_DIM
ATT_KV = ATT_KV_HEADS * ATT_HEAD_DIM
HGRN_HEADS = 8
HGRN_DK = 128
HG_K = HGRN_HEADS * HGRN_DK
HG_V = HG_K
IN_COLS = ATT_Q + 2 * ATT_KV + 3 * HG_K + 2 * HG_V + 2 * D_MODEL
PEER_HEADS = 8
PEER_NKEYS = 128
PEER_HALF = 128
PEER_TOPK = 16
PEER_SEL = PEER_HEADS * PEER_TOPK
EPS = 1e-6

COL_QA = 0
COL_KA = COL_QA + ATT_Q
COL_VA = COL_KA + ATT_KV
COL_QH = COL_VA + ATT_KV
COL_ZF = COL_QH + HG_K
COL_ZB = COL_ZF + HG_K
COL_IH = COL_ZB + HG_K
COL_GH = COL_IH + HG_V
COL_GA = COL_GH + HG_V
COL_GB = COL_GA + D_MODEL

LANES = 128
SUBLANES = 8
VMEM_LIMIT = 56 * 1024 * 1024

HG_CHUNK = 128
HG_SUB = SUBLANES
HG_LEVELS = (64, 32, 16, 8)


def _cparams(sem):
    return pltpu.CompilerParams(dimension_semantics=sem, vmem_limit_bytes=VMEM_LIMIT)


def _nt_dot(a, b):
    return lax.dot_general(a, b, (((1,), (1,)), ((), ())), preferred_element_type=F32)


def _tn_dot(a, b):
    return lax.dot_general(a, b, (((0,), (0,)), ((), ())), preferred_element_type=F32)


def _mod_kernel(c_ref, w_ref, b_ref, o_ref):
    c = c_ref[...]
    sc = c * jax.nn.sigmoid(c)
    o_ref[...] = jnp.dot(sc, w_ref[...], preferred_element_type=F32, precision=HIGHEST) + b_ref[...]


def _modulation(c, w_mod, b_mod):
    B = c.shape[0]
    n = w_mod.shape[1]
    tn = D_MODEL
    return pl.pallas_call(
        _mod_kernel,
        grid=(n // tn,),
        in_specs=[
            pl.BlockSpec((B, D_MODEL), lambda j: (0, 0)),
            pl.BlockSpec((D_MODEL, tn), lambda j: (0, j)),
            pl.BlockSpec((1, tn), lambda j: (0, j)),
        ],
        out_specs=pl.BlockSpec((B, tn), lambda j: (0, j)),
        out_shape=jax.ShapeDtypeStruct((B, n), F32),
        compiler_params=_cparams(("parallel",)),
    )(c, w_mod, b_mod.reshape(1, n))


def _inproj_kernel(x_ref, mod_ref, g_ref, w_ref, o_ref, h_scr):
    @pl.when(pl.program_id(2) == 0)
    def _():
        x = x_ref[...]
        y = x * lax.rsqrt(jnp.mean(x * x, axis=-1, keepdims=True) + EPS) * g_ref[...]
        h = y * (1.0 + mod_ref[1:2, :]) + mod_ref[0:1, :]
        h_scr[...] = h.astype(BF16)

    o_ref[...] = jnp.dot(h_scr[...], w_ref[...], preferred_element_type=F32)


def _inproj(x, mod3, g1, w_in_bf):
    B, S, _ = x.shape
    tm = 512
    tn = 2176
    return pl.pallas_call(
        _inproj_kernel,
        grid=(B, S // tm, IN_COLS // tn),
        in_specs=[
            pl.BlockSpec((None, tm, D_MODEL), lambda b, i, j: (b, i, 0)),
            pl.BlockSpec((None, 6, D_MODEL), lambda b, i, j: (b, 0, 0)),
            pl.BlockSpec((1, D_MODEL), lambda b, i, j: (0, 0)),
            pl.BlockSpec((D_MODEL, tn), lambda b, i, j: (0, j)),
        ],
        out_specs=pl.BlockSpec((None, tm, tn), lambda b, i, j: (b, i, j)),
        out_shape=jax.ShapeDtypeStruct((B, S, IN_COLS), F32),
        scratch_shapes=[pltpu.VMEM((tm, D_MODEL), BF16)],
        compiler_params=_cparams(("parallel", "parallel", "arbitrary")),
    )(x, mod3, g1, w_in_bf)


def _attn_kernel(sink_ref, q_ref, kp_ref, kc_ref, kn_ref, vp_ref, vc_ref, vn_ref, o_ref, *, nblk):
    j = pl.program_id(1)
    blk = WINDOW
    span = 3 * blk
    q = q_ref[...].astype(BF16)
    k = jnp.concatenate([kp_ref[...], kc_ref[...], kn_ref[...]], axis=0).astype(BF16)
    v = jnp.concatenate([vp_ref[...], vc_ref[...], vn_ref[...]], axis=0).astype(BF16)
    qi = lax.broadcasted_iota(jnp.int32, (blk, span), 0)
    kj = lax.broadcasted_iota(jnp.int32, (blk, span), 1)
    dist = jnp.abs(kj - blk - qi)
    lo = jnp.where(j == 0, blk, 0)
    hi = jnp.where(j == nblk - 1, 2 * blk, span)
    valid = (dist <= WINDOW) & (kj >= lo) & (kj < hi)
    distf = dist.astype(F32)
    scale = 1.0 / math.sqrt(ATT_HEAD_DIM)
    outs = []
    for hh in range(ATT_HEADS):
        h = hh // ATT_GROUP
        slope = 2.0 ** (-8.0 * (hh + 1) / ATT_HEADS)
        qh = q[:, hh * ATT_HEAD_DIM:(hh + 1) * ATT_HEAD_DIM]
        kh = k[:, h * ATT_HEAD_DIM:(h + 1) * ATT_HEAD_DIM]
        vh = v[:, h * ATT_HEAD_DIM:(h + 1) * ATT_HEAD_DIM]
        s = _nt_dot(qh, kh) * scale - slope * distf
        s = jnp.where(valid, s, -jnp.inf)
        sink = sink_ref[hh]
        m = jnp.maximum(jnp.max(s, axis=-1, keepdims=True), sink)
        p = jnp.exp(s - m)
        denom = jnp.sum(p, axis=-1, keepdims=True) + jnp.exp(sink - m)
        o = jnp.dot(p.astype(BF16), vh, preferred_element_type=F32) / denom
        outs.append(o)
    o_ref[...] = jnp.concatenate(outs, axis=-1).astype(o_ref.dtype)


def _attention(proj, sink):
    B, S, _ = proj.shape
    blk = WINDOW
    nblk = S // blk
    kcol = COL_KA // ATT_KV
    vcol = COL_VA // ATT_KV

    def kv_spec(col, off):
        return pl.BlockSpec(
            (None, blk, ATT_KV),
            lambda b, j: (b, jnp.clip(j + off, 0, nblk - 1), col))

    return pl.pallas_call(
        functools.partial(_attn_kernel, nblk=nblk),
        grid=(B, nblk),
        in_specs=[
            pl.BlockSpec(memory_space=pltpu.SMEM),
            pl.BlockSpec((None, blk, ATT_Q), lambda b, j: (b, j, 0)),
            kv_spec(kcol, -1), kv_spec(kcol, 0), kv_spec(kcol, 1),
            kv_spec(vcol, -1), kv_spec(vcol, 0), kv_spec(vcol, 1),
        ],
        out_specs=pl.BlockSpec((None, blk, ATT_Q), lambda b, j: (b, j, 0)),
        out_shape=jax.ShapeDtypeStruct((B, S, ATT_Q), BF16),
        compiler_params=_cparams(("parallel", "parallel")),
    )(sink, proj, proj, proj, proj, proj, proj, proj)


def _hgrn_chunk(q_raw, z, v, lb, st_scr, reverse):
    C = HG_CHUNK
    q = q_raw * jax.nn.sigmoid(q_raw)
    f = lb + (1.0 - lb) * jax.nn.sigmoid(z)
    kk = (1.0 - lb) * jax.nn.sigmoid(-z)
    logf = jnp.log(f)
    row = lax.broadcasted_iota(jnp.int32, (C, C), 0)
    col = lax.broadcasted_iota(jnp.int32, (C, C), 1)
    tri = jnp.where((row <= col) if reverse else (row >= col), 1.0, 0.0).astype(F32)
    b = jnp.dot(tri, logf, preferred_element_type=F32, precision=HIGHEST)
    edge = b[0:1, :] if reverse else b[C - 1:C, :]
    vb = v.astype(BF16)

    st = st_scr[...]
    o = _nt_dot((q * jnp.exp(b)).astype(BF16), st.astype(BF16))
    kh = (kk * jnp.exp(edge - b)).astype(BF16)
    st_scr[...] = st * jnp.exp(edge) + _tn_dot(vb, kh)

    pos = lax.broadcasted_iota(jnp.int32, (C, 1), 0)
    a = jnp.zeros((C, C), F32)
    for hsz in HG_LEVELS:
        seg = 2 * hsz
        refs = []
        for s0 in range(0, C, seg):
            r = s0 + hsz if reverse else s0 + hsz - 1
            refs.append(jnp.broadcast_to(b[r:r + 1, :], (seg, HGRN_DK)))
        bmid = jnp.concatenate(refs, axis=0) if len(refs) > 1 else refs[0]
        e = jnp.exp(-jnp.abs(b - bmid))
        upper = (pos % seg) >= hsz
        qrole = jnp.logical_not(upper) if reverse else upper
        ql = jnp.where(qrole, q * e, 0.0).astype(BF16)
        kl = jnp.where(qrole, 0.0, kk * e).astype(BF16)
        al = _nt_dot(ql, kl)
        same = (row // seg) == (col // seg)
        rq = (row % seg) >= hsz
        ck = (col % seg) >= hsz
        if reverse:
            m = same & jnp.logical_not(rq) & ck
        else:
            m = same & rq & jnp.logical_not(ck)
        a = jnp.where(m, al, a)
    o = o + jnp.dot(a.astype(BF16), vb, preferred_element_type=F32)

    nb = C // HG_SUB
    shape3 = (nb, HG_SUB, HGRN_DK)
    b3 = b.reshape(shape3)
    q3 = q.reshape(shape3)
    k3 = kk.reshape(shape3)
    v3 = v.reshape(shape3)
    sub = lax.broadcasted_iota(jnp.int32, shape3, 1)
    acc = jnp.zeros(shape3, F32)
    for t in range(HG_SUB):
        arg = jnp.minimum(b3[:, t:t + 1, :] - b3, 0.0)
        keep = (sub >= t) if reverse else (sub <= t)
        g = jnp.where(keep, q3[:, t:t + 1, :] * k3 * jnp.exp(arg), 0.0)
        w = jnp.sum(g, axis=-1, keepdims=True)
        r = jnp.sum(w * v3, axis=1, keepdims=True)
        acc = jnp.where(sub == t, r, acc)
    return o + acc.reshape(C, HGRN_DK)


def _lower_bound(plb_ref):
    p = plb_ref[...]
    e = jnp.exp(p - jnp.max(p, axis=0, keepdims=True))
    return e[0:1, :] / jnp.sum(e, axis=0, keepdims=True)


def _hgrn_fwd_kernel(q_ref, z_ref, v_ref, plb_ref, o_ref, st_scr, *, nchunk):
    @pl.when(pl.program_id(2) == 0)
    def _():
        st_scr[...] = jnp.zeros_like(st_scr)

    lb = _lower_bound(plb_ref)
    C = HG_CHUNK
    for ci in range(nchunk):
        sl = slice(ci * C, (ci + 1) * C)
        o_ref[sl, :] = _hgrn_chunk(q_ref[sl, :], z_ref[sl, :], v_ref[sl, :], lb, st_scr, False)


def _hgrn_bwd_kernel(q_ref, z_ref, v_ref, plb_ref, of_ref, gh_ref, gn_ref, o_ref, st_scr, *, nchunk):
    @pl.when(pl.program_id(2) == 0)
    def _():
        st_scr[...] = jnp.zeros_like(st_scr)

    lb = _lower_bound(plb_ref)
    C = HG_CHUNK
    for ci in reversed(range(nchunk)):
        sl = slice(ci * C, (ci + 1) * C)
        o = _hgrn_chunk(q_ref[sl, :], z_ref[sl, :], v_ref[sl, :], lb, st_scr, True)
        o = o + of_ref[sl, :]
        o = o * lax.rsqrt(jnp.mean(o * o, axis=-1, keepdims=True) + EPS) * gn_ref[...]
        gh = gh_ref[sl, :]
        o_ref[sl, :] = (o * (gh * jax.nn.sigmoid(gh))).astype(o_ref.dtype)


def _hgrn(proj, p_lb_fwd, p_lb_bwd, g_hnorm):
    B, S, _ = proj.shape
    tb = 512
    nt = S // tb
    nchunk = tb // HG_CHUNK
    dk = HGRN_DK
    scratch = [pltpu.VMEM((dk, dk), F32)]

    def col_spec(col0, rev):
        if rev:
            return pl.BlockSpec((None, tb, dk), lambda b, h, t: (b, nt - 1 - t, col0 // dk + h))
        return pl.BlockSpec((None, tb, dk), lambda b, h, t: (b, t, col0 // dk + h))

    plb_spec = pl.BlockSpec((p_lb_fwd.shape[0], dk), lambda b, h, t: (0, h))
    sem = ("parallel", "parallel", "arbitrary")
    o_f = pl.pallas_call(
        functools.partial(_hgrn_fwd_kernel, nchunk=nchunk),
        grid=(B, HGRN_HEADS, nt),
        in_specs=[col_spec(COL_QH, False), col_spec(COL_ZF, False), col_spec(COL_IH, False), plb_spec],
        out_specs=pl.BlockSpec((None, tb, dk), lambda b, h, t: (b, t, h)),
        out_shape=jax.ShapeDtypeStruct((B, S, HG_V), F32),
        scratch_shapes=scratch,
        compiler_params=_cparams(sem),
    )(proj, proj, proj, p_lb_fwd)
    return pl.pallas_call(
        functools.partial(_hgrn_bwd_kernel, nchunk=nchunk),
        grid=(B, HGRN_HEADS, nt),
        in_specs=[col_spec(COL_QH, True), col_spec(COL_ZB, True), col_spec(COL_IH, True), plb_spec,
                  pl.BlockSpec((None, tb, dk), lambda b, h, t: (b, nt - 1 - t, h)),
                  col_spec(COL_GH, True),
                  pl.BlockSpec((1, dk), lambda b, h, t: (0, h))],
        out_specs=pl.BlockSpec((None, tb, dk), lambda b, h, t: (b, nt - 1 - t, h)),
        out_shape=jax.ShapeDtypeStruct((B, S, HG_V), BF16),
        scratch_shapes=scratch,
        compiler_params=_cparams(sem),
    )(proj, proj, proj, p_lb_bwd, o_f, proj, g_hnorm)


def _outproj_kernel(att_ref, hg_ref, ga0_ref, ga1_ref, gb0_ref, gb1_ref, x_ref, mod_ref, woa_ref, woh_ref,
                    wout_ref, g2_ref, wpq_ref, x1_ref, h2_ref, qp_ref):
    ya = jnp.dot(att_ref[...], woa_ref[...], preferred_element_type=F32)
    yh = jnp.dot(hg_ref[...], woh_ref[...], preferred_element_type=F32)
    ga = jnp.concatenate([ga0_ref[...], ga1_ref[...]], axis=-1)
    gb = jnp.concatenate([gb0_ref[...], gb1_ref[...]], axis=-1)
    merged = jax.nn.sigmoid(ga) * ya + jax.nn.sigmoid(gb) * yh
    x1 = x_ref[...] + mod_ref[2:3, :] * jnp.dot(merged.astype(BF16), wout_ref[...],
                                                preferred_element_type=F32)
    x1_ref[...] = x1
    y = x1 * lax.rsqrt(jnp.mean(x1 * x1, axis=-1, keepdims=True) + EPS) * g2_ref[...]
    h2 = y * (1.0 + mod_ref[4:5, :]) + mod_ref[3:4, :]
    h2_ref[...] = h2
    qp_ref[...] = jnp.dot(h2.astype(BF16), wpq_ref[...], preferred_element_type=F32)


def _outproj(att, hg, proj, x, mod3, woa, woh, wout, g2, wpq):
    B, S, _ = x.shape
    tm = 256
    half = D_MODEL // 2
    nq = wpq.shape[1]

    def tok(width, col):
        return pl.BlockSpec((None, tm, width), lambda b, i: (b, i, col))

    def full(shape):
        return pl.BlockSpec(shape, lambda b, i: (0, 0))

    return pl.pallas_call(
        _outproj_kernel,
        grid=(B, S // tm),
        in_specs=[
            tok(ATT_Q, 0), tok(HG_V, 0),
            tok(half, COL_GA // half), tok(half, COL_GA // half + 1),
            tok(half, COL_GB // half), tok(half, COL_GB // half + 1),
            tok(D_MODEL, 0),
            pl.BlockSpec((None, 6, D_MODEL), lambda b, i: (b, 0, 0)),
            full((ATT_Q, D_MODEL)), full((HG_V, D_MODEL)), full((D_MODEL, D_MODEL)),
            full((1, D_MODEL)), full((D_MODEL, nq)),
        ],
        out_specs=[tok(D_MODEL, 0), tok(D_MODEL, 0), tok(nq, 0)],
        out_shape=[jax.ShapeDtypeStruct((B, S, D_MODEL), F32),
                   jax.ShapeDtypeStruct((B, S, D_MODEL), F32),
                   jax.ShapeDtypeStruct((B, S, nq), F32)],
        compiler_params=_cparams(("parallel", "parallel")),
    )(att, hg, proj, proj, proj, proj, x, mod3, woa, woh, wout, g2, wpq)


def _route_kernel(q_ref, keys_ref, eid_ref, gate_ref, sv_scr, si_scr, cand_scr, cid_scr, ts_scr):
    tt = q_ref.shape[0]
    n = PEER_NKEYS
    k = PEER_TOPK
    q = q_ref[...].astype(BF16)
    iota_n = lax.broadcasted_iota(jnp.int32, (n, tt), 0)
    for p in range(2):
        s = _nt_dot(keys_ref[p * n:(p + 1) * n, :].astype(BF16), q[:, p * PEER_HALF:(p + 1) * PEER_HALF])
        for i in range(k):
            m = jnp.max(s, axis=0, keepdims=True)
            idx = jnp.min(jnp.where(s == m, iota_n, n), axis=0, keepdims=True)
            sv_scr[p, i:i + 1, :] = m
            si_scr[p, i:i + 1, :] = idx
            s = jnp.where(iota_n == idx, -jnp.inf, s)
    ncand = cand_scr.shape[0]
    cand_scr[...] = jnp.full(cand_scr.shape, -jnp.inf, F32)
    cid_scr[...] = jnp.zeros(cid_scr.shape, jnp.int32)
    off = 0
    for a in range(k):
        nb = k // (a + 1)
        cand_scr[off:off + nb, :] = sv_scr[0, a:a + 1, :] + sv_scr[1, 0:nb, :]
        cid_scr[off:off + nb, :] = si_scr[0, a:a + 1, :] * n + si_scr[1, 0:nb, :]
        off += nb
    c = cand_scr[...]
    ids = cid_scr[...]
    iota_c = lax.broadcasted_iota(jnp.int32, (ncand, tt), 0)
    for i in range(k):
        m = jnp.max(c, axis=0, keepdims=True)
        pos = jnp.min(jnp.where(c == m, iota_c, ncand), axis=0, keepdims=True)
        sel = iota_c == pos
        eid_ref[i:i + 1, :] = jnp.max(jnp.where(sel, ids, -1), axis=0, keepdims=True)
        ts_scr[i:i + 1, :] = m
        c = jnp.where(sel, -jnp.inf, c)
    ts = ts_scr[...]
    ex = jnp.exp(ts - ts[0:1, :])
    gate_ref[...] = ex / jnp.sum(ex, axis=0, keepdims=True)


def _route(qp, keys2d):
    n_tok = qp.shape[0]
    tt = 256
    k = PEER_TOPK
    qd = 2 * PEER_HALF
    npairs = sum(k // (a + 1) for a in range(k))
    ncand = -(-npairs // SUBLANES) * SUBLANES
    return pl.pallas_call(
        _route_kernel,
        grid=(n_tok // tt, PEER_HEADS),
        in_specs=[
            pl.BlockSpec((tt, qd), lambda i, h: (i, h)),
            pl.BlockSpec((2 * PEER_NKEYS, PEER_HALF), lambda i, h: (h, 0)),
        ],
        out_specs=[pl.BlockSpec((k, tt), lambda i, h: (h, i)),
                   pl.BlockSpec((k, tt), lambda i, h: (h, i))],
        out_shape=[jax.ShapeDtypeStruct((PEER_SEL, n_tok), jnp.int32),
                   jax.ShapeDtypeStruct((PEER_SEL, n_tok), F32)],
        scratch_shapes=[pltpu.VMEM((2, k, tt), F32), pltpu.VMEM((2, k, tt), jnp.int32),
                        pltpu.VMEM((ncand, tt), F32), pltpu.VMEM((ncand, tt), jnp.int32),
                        pltpu.VMEM((k, tt), F32)],
        compiler_params=_cparams(("parallel", "arbitrary")),
    )(qp, keys2d)


PEER_GROUP = SUBLANES
PEER_NGROUP = 4
PEER_TT = PEER_GROUP * PEER_NGROUP
U_MASK = 0xFFFF0000


PEER_CHUNKS = D_MODEL // LANES
PEER_SEL_TILES = PEER_SEL // SUBLANES


def _pack_tables(u, v):
    ub = lax.bitcast_convert_type(u.astype(BF16), jnp.uint16).astype(jnp.uint32)
    vb = lax.bitcast_convert_type(v.astype(BF16), jnp.uint16).astype(jnp.uint32)
    return ((ub << 16) | vb).reshape(u.shape[0], PEER_CHUNKS, 1, LANES)


def _peer_kernel(eid_ref, eidn_ref, gate_ref, h2_ref, x1_ref, mod_ref, gf_ref, tab_hbm, o_ref, wbuf, sem):
    i = pl.program_id(0)
    n = pl.num_programs(0)
    slot = i % 2
    G = PEER_GROUP

    def issue_group(idx_ref, g, s):
        def token(r, carry):
            tok = g * G + r
            for j in range(PEER_SEL):
                e = idx_ref[tok, j]
                dst = wbuf.at[s, tok, j // SUBLANES, pl.ds(0, PEER_CHUNKS), pl.ds(j % SUBLANES, 1)]
                pltpu.make_async_copy(tab_hbm.at[e], dst, sem.at[s, g]).start(priority=j % 2)
            return carry
        lax.fori_loop(0, G, token, 0)

    def wait_group(g, s):
        rows = pl.ds(g * G, G)
        pltpu.make_async_copy(wbuf.at[s, rows], wbuf.at[s, rows], sem.at[s, g]).wait()

    def compute_group(g, s):
        rows = pl.ds(pl.multiple_of(g * G, G), G)
        gate_t = gate_ref[rows, :].T
        h2g = h2_ref[rows, :]
        sub = lax.broadcasted_iota(jnp.int32, (G, D_MODEL), 0)
        y = jnp.zeros((G, D_MODEL), F32)
        for r in range(G):
            tok = g * G + r
            acc = jnp.zeros((PEER_SEL_TILES, SUBLANES, LANES), F32)
            for c in range(PEER_CHUNKS):
                uf = lax.bitcast_convert_type(wbuf[s, tok, :, c] & jnp.uint32(U_MASK), F32)
                acc = acc + uf * h2g[r:r + 1, c * LANES:(c + 1) * LANES]
            act = jnp.sum(acc, axis=-1, keepdims=True)
            act = 0.5 * act * (1.0 + lax.erf(act * (1.0 / math.sqrt(2.0))))
            w = gate_t[:, r:r + 1].reshape(PEER_SEL_TILES, SUBLANES, 1) * act
            pieces = []
            for c in range(PEER_CHUNKS):
                vf = lax.bitcast_convert_type(wbuf[s, tok, :, c] << 16, F32)
                pieces.append(jnp.sum(jnp.sum(w * vf, axis=0), axis=0, keepdims=True))
            y = jnp.where(sub == r, jnp.concatenate(pieces, axis=-1), y)
        x2 = x1_ref[rows, :] + mod_ref[5:6, :] * y
        o_ref[rows, :] = x2 * lax.rsqrt(jnp.mean(x2 * x2, axis=-1, keepdims=True) + EPS) * gf_ref[...]

    @pl.when(i == 0)
    def _():
        def first(g, carry):
            issue_group(eid_ref, g, 0)
            return carry
        lax.fori_loop(0, PEER_NGROUP, first, 0)

    def step(g, carry):
        @pl.when(i + 1 < n)
        def _():
            issue_group(eidn_ref, g, 1 - slot)

        wait_group(g, slot)
        compute_group(g, slot)
        return carry

    lax.fori_loop(0, PEER_NGROUP, step, 0)


def _peer(eid_t, gate_t, h2, x1, mod3, g_final, table, seq):
    n_tok = h2.shape[0]
    tt = PEER_TT
    n = n_tok // tt
    smem_idx = lambda f: pl.BlockSpec((tt, PEER_SEL), f, memory_space=pltpu.SMEM)
    tok = lambda w: pl.BlockSpec((tt, w), lambda i: (i, 0))
    return pl.pallas_call(
        _peer_kernel,
        grid=(n,),
        in_specs=[
            smem_idx(lambda i: (i, 0)),
            smem_idx(lambda i: (jnp.minimum(i + 1, n - 1), 0)),
            tok(PEER_SEL), tok(D_MODEL), tok(D_MODEL),
            pl.BlockSpec((None, 6, D_MODEL), lambda i: ((i * tt) // seq, 0, 0)),
            pl.BlockSpec((1, D_MODEL), lambda i: (0, 0)),
            pl.BlockSpec(memory_space=pl.ANY),
        ],
        out_specs=tok(D_MODEL),
        out_shape=jax.ShapeDtypeStruct((n_tok, D_MODEL), F32),
        scratch_shapes=[pltpu.VMEM((2, tt, PEER_SEL_TILES, PEER_CHUNKS, SUBLANES, LANES), jnp.uint32),
                        pltpu.SemaphoreType.DMA((2, PEER_NGROUP))],
        compiler_params=_cparams(("arbitrary",)),
    )(eid_t, eid_t, gate_t, h2, x1, mod3, g_final, table)


def _run_group(x, c, w_mod, b_mod, g1, w_in_bf, sink, p_lb_fwd, p_lb_bwd, g_hnorm, woa, woh, wout, g2,
               wpq, keys2d, table, g_final):
    B, S, _ = x.shape
    mod3 = _modulation(c, w_mod, b_mod).reshape(B, 6, D_MODEL)
    proj = _inproj(x, mod3, g1, w_in_bf)
    att = _attention(proj, sink)
    hg = _hgrn(proj, p_lb_fwd, p_lb_bwd, g_hnorm)
    x1, h2, qp = _outproj(att, hg, proj, x, mod3, woa, woh, wout, g2, wpq)
    n_tok = B * S
    eid, gate = _route(qp.reshape(n_tok, -1), keys2d)
    y = _peer(eid.T, gate.T, h2.reshape(n_tok, D_MODEL), x1.reshape(n_tok, D_MODEL), mod3, g_final, table, S)
    return y.reshape(B, S, D_MODEL)


def kernel(x_prompt, x_sample, c_prompt, c_sample, w_mod, b_mod, g_norm1, w_in, att_sink, p_lb_fwd, p_lb_bwd,
           g_hnorm, w_o_att, w_o_hgrn, w_out, g_norm2, w_pq, peer_keys, peer_u, peer_v, g_final):
    assert w_mod.shape[0] == 1 and p_lb_fwd.shape[0] == 2, "single-layer encoder"
    shared = (
        w_mod[0], b_mod[0], g_norm1[0].reshape(1, D_MODEL), w_in[0].astype(BF16), att_sink[0],
        p_lb_fwd, p_lb_bwd, g_hnorm[0].reshape(1, HG_V),
        w_o_att[0].astype(BF16), w_o_hgrn[0].astype(BF16), w_out[0].astype(BF16),
        g_norm2[0].reshape(1, D_MODEL), w_pq[0].astype(BF16),
        peer_keys[0].reshape(PEER_HEADS * 2 * PEER_NKEYS, PEER_HALF),
        _pack_tables(peer_u[0], peer_v[0]), g_final.reshape(1, D_MODEL),
    )
    return (_run_group(x_prompt, c_prompt, *shared), _run_group(x_sample, c_sample, *shared))
```

```python
import functools
import math

import jax
import jax.numpy as jnp
from jax import lax
from jax.experimental import pallas as pl
from jax.experimental.pallas import tpu as pltpu

F32 = jnp.float32
BF16 = jnp.bfloat16
HIGHEST = lax.Precision.HIGHEST

D_MODEL = 1024
ATT_HEADS = 16
ATT_KV_HEADS = 4
ATT_HEAD_DIM = 64
ATT_GROUP = ATT_HEADS // ATT_KV_HEADS
WINDOW = 128
ATT_Q = ATT_HEADS * ATT_HEAD_DIM
ATT_KV = ATT_KV_HEADS * ATT_HEAD_DIM
HGRN_HEADS = 8
HGRN_DK = 128
HG_K = HGRN_HEADS * HGRN_DK
HG_V = HG_K
IN_COLS = ATT_Q + 2 * ATT_KV + 3 * HG_K + 2 * HG_V + 2 * D_MODEL
PEER_HEADS = 8
PEER_NKEYS = 128
PEER_HALF = 128
PEER_TOPK = 16
PEER_SEL = PEER_HEADS * PEER_TOPK
EPS = 1e-6

COL_QA = 0
COL_KA = COL_QA + ATT_Q
COL_VA = COL_KA + ATT_KV
COL_QH = COL_VA + ATT_KV
COL_ZF = COL_QH + HG_K
COL_ZB = COL_ZF + HG_K
COL_IH = COL_ZB + HG_K
COL_GH = COL_IH + HG_V
COL_GA = COL_GH + HG_V
COL_GB = COL_GA + D_MODEL

LANES = 128
SUBLANES = 8
VMEM_LIMIT = 56 * 1024 * 1024

HG_CHUNK = 128
HG_SUB = SUBLANES
HG_LEVELS = (64, 32, 16, 8)


def _cparams(sem):
    return pltpu.CompilerParams(dimension_semantics=sem, vmem_limit_bytes=VMEM_LIMIT)


def _nt_dot(a, b):
    return lax.dot_general(a, b, (((1,), (1,)), ((), ())), preferred_element_type=F32)


def _tn_dot(a, b):
    return lax.dot_general(a, b, (((0,), (0,)), ((), ())), preferred_element_type=F32)


def _mod_kernel(c_ref, w_ref, b_ref, o_ref):
    c = c_ref[...]
    sc = c * jax.nn.sigmoid(c)
    o_ref[...] = jnp.dot(sc, w_ref[...], preferred_element_type=F32, precision=HIGHEST) + b_ref[...]


def _modulation(c, w_mod, b_mod):
    B = c.shape[0]
    n = w_mod.shape[1]
    tn = D_MODEL
    return pl.pallas_call(
        _mod_kernel,
        grid=(n // tn,),
        in_specs=[
            pl.BlockSpec((B, D_MODEL), lambda j: (0, 0)),
            pl.BlockSpec((D_MODEL, tn), lambda j: (0, j)),
            pl.BlockSpec((1, tn), lambda j: (0, j)),
        ],
        out_specs=pl.BlockSpec((B, tn), lambda j: (0, j)),
        out_shape=jax.ShapeDtypeStruct((B, n), F32),
        compiler_params=_cparams(("parallel",)),
    )(c, w_mod, b_mod.reshape(1, n))


def _inproj_kernel(x_ref, mod_ref, g_ref, w_ref, o_ref, h_scr):
    @pl.when(pl.program_id(2) == 0)
    def _():
        x = x_ref[...]
        y = x * lax.rsqrt(jnp.mean(x * x, axis=-1, keepdims=True) + EPS) * g_ref[...]
        h = y * (1.0 + mod_ref[1:2, :]) + mod_ref[0:1, :]
        h_scr[...] = h.astype(BF16)

    o_ref[...] = jnp.dot(h_scr[...], w_ref[...], preferred_element_type=F32)


def _inproj(x, mod3, g1, w_in_bf):
    B, S, _ = x.shape
    tm = 512
    tn = 2176
    return pl.pallas_call(
        _inproj_kernel,
        grid=(B, S // tm, IN_COLS // tn),
        in_specs=[
            pl.BlockSpec((None, tm, D_MODEL), lambda b, i, j: (b, i, 0)),
            pl.BlockSpec((None, 6, D_MODEL), lambda b, i, j: (b, 0, 0)),
            pl.BlockSpec((1, D_MODEL), lambda b, i, j: (0, 0)),
            pl.BlockSpec((D_MODEL, tn), lambda b, i, j: (0, j)),
        ],
        out_specs=pl.BlockSpec((None, tm, tn), lambda b, i, j: (b, i, j)),
        out_shape=jax.ShapeDtypeStruct((B, S, IN_COLS), F32),
        scratch_shapes=[pltpu.VMEM((tm, D_MODEL), BF16)],
        compiler_params=_cparams(("parallel", "parallel", "arbitrary")),
    )(x, mod3, g1, w_in_bf)


def _attn_kernel(sink_ref, q_ref, kp_ref, kc_ref, kn_ref, vp_ref, vc_ref, vn_ref, o_ref, *, nblk):
    j = pl.program_id(1)
    blk = WINDOW
    span = 3 * blk
    q = q_ref[...].astype(BF16)
    k = jnp.concatenate([kp_ref[...], kc_ref[...], kn_ref[...]], axis=0).astype(BF16)
    v = jnp.concatenate([vp_ref[...], vc_ref[...], vn_ref[...]], axis=0).astype(BF16)
    qi = lax.broadcasted_iota(jnp.int32, (blk, span), 0)
    kj = lax.broadcasted_iota(jnp.int32, (blk, span), 1)
    dist = jnp.abs(kj - blk - qi)
    lo = jnp.where(j == 0, blk, 0)
    hi = jnp.where(j == nblk - 1, 2 * blk, span)
    valid = (dist <= WINDOW) & (kj >= lo) & (kj < hi)
    distf = dist.astype(F32)
    scale = 1.0 / math.sqrt(ATT_HEAD_DIM)
    outs = []
    for hh in range(ATT_HEADS):
        h = hh // ATT_GROUP
        slope = 2.0 ** (-8.0 * (hh + 1) / ATT_HEADS)
        qh = q[:, hh * ATT_HEAD_DIM:(hh + 1) * ATT_HEAD_DIM]
        kh = k[:, h * ATT_HEAD_DIM:(h + 1) * ATT_HEAD_DIM]
        vh = v[:, h * ATT_HEAD_DIM:(h + 1) * ATT_HEAD_DIM]
        s = _nt_dot(qh, kh) * scale - slope * distf
        s = jnp.where(valid, s, -jnp.inf)
        sink = sink_ref[hh]
        m = jnp.maximum(jnp.max(s, axis=-1, keepdims=True), sink)
        p = jnp.exp(s - m)
        denom = jnp.sum(p, axis=-1, keepdims=True) + jnp.exp(sink - m)
        o = jnp.dot(p.astype(BF16), vh, preferred_element_type=F32) / denom
        outs.append(o)
    o_ref[...] = jnp.concatenate(outs, axis=-1).astype(o_ref.dtype)


def _attention(proj, sink):
    B, S, _ = proj.shape
    blk = WINDOW
    nblk = S // blk
    kcol = COL_KA // ATT_KV
    vcol = COL_VA // ATT_KV

    def kv_spec(col, off):
        return pl.BlockSpec(
            (None, blk, ATT_KV),
            lambda b, j: (b, jnp.clip(j + off, 0, nblk - 1), col))

    return pl.pallas_call(
        functools.partial(_attn_kernel, nblk=nblk),
        grid=(B, nblk),
        in_specs=[
            pl.BlockSpec(memory_space=pltpu.SMEM),
            pl.BlockSpec((None, blk, ATT_Q), lambda b, j: (b, j, 0)),
            kv_spec(kcol, -1), kv_spec(kcol, 0), kv_spec(kcol, 1),
            kv_spec(vcol, -1), kv_spec(vcol, 0), kv_spec(vcol, 1),
        ],
        out_specs=pl.BlockSpec((None, blk, ATT_Q), lambda b, j: (b, j, 0)),
        out_shape=jax.ShapeDtypeStruct((B, S, ATT_Q), BF16),
        compiler_params=_cparams(("parallel", "parallel")),
    )(sink, proj, proj, proj, proj, proj, proj, proj)


def _hgrn_chunk(q_raw, z, v, lb, st_scr, reverse):
    C = HG_CHUNK
    q = q_raw * jax.nn.sigmoid(q_raw)
    f = lb + (1.0 - lb) * jax.nn.sigmoid(z)
    kk = (1.0 - lb) * jax.nn.sigmoid(-z)
    logf = jnp.log(f)
    row = lax.broadcasted_iota(jnp.int32, (C, C), 0)
    col = lax.broadcasted_iota(jnp.int32, (C, C), 1)
    tri = jnp.where((row <= col) if reverse else (row >= col), 1.0, 0.0).astype(F32)
    b = jnp.dot(tri, logf, preferred_element_type=F32, precision=HIGHEST)
    edge = b[0:1, :] if reverse else b[C - 1:C, :]
    vb = v.astype(BF16)

    st = st_scr[...]
    o = _nt_dot((q * jnp.exp(b)).astype(BF16), st.astype(BF16))
    kh = (kk * jnp.exp(edge - b)).astype(BF16)
    st_scr[...] = st * jnp.exp(edge) + _tn_dot(vb, kh)

    pos = lax.broadcasted_iota(jnp.int32, (C, 1), 0)
    a = jnp.zeros((C, C), F32)
    for hsz in HG_LEVELS:
        seg = 2 * hsz
        refs = []
        for s0 in range(0, C, seg):
            r = s0 + hsz if reverse else s0 + hsz - 1
            refs.append(jnp.broadcast_to(b[r:r + 1, :], (seg, HGRN_DK)))
        bmid = jnp.concatenate(refs, axis=0) if len(refs) > 1 else refs[0]
        e = jnp.exp(-jnp.abs(b - bmid))
        upper = (pos % seg) >= hsz
        qrole = jnp.logical_not(upper) if reverse else upper
        ql = jnp.where(qrole, q * e, 0.0).astype(BF16)
        kl = jnp.where(qrole, 0.0, kk * e).astype(BF16)
        al = _nt_dot(ql, kl)
        same = (row // seg) == (col // seg)
        rq = (row % seg) >= hsz
        ck = (col % seg) >= hsz
        if reverse:
            m = same & jnp.logical_not(rq) & ck
        else:
            m = same & rq & jnp.logical_not(ck)
        a = jnp.where(m, al, a)
    nb = C // HG_SUB
    shape3 = (nb, HG_SUB, HGRN_DK)
    b3 = b.reshape(shape3)
    q3 = q.reshape(shape3)
    k3 = kk.reshape(shape3)
    sub = lax.broadcasted_iota(jnp.int32, shape3, 1)
    block_col0 = (row // HG_SUB) * HG_SUB
    for s in range(HG_SUB):
        arg = jnp.minimum(b3 - b3[:, s:s + 1, :], 0.0)
        keep = (sub <= s) if reverse else (sub >= s)
        g = jnp.where(keep, q3 * k3[:, s:s + 1, :] * jnp.exp(arg), 0.0)
        w = jnp.sum(g, axis=-1, keepdims=True).reshape(C, 1)
        a = jnp.where(col == block_col0 + s, w, a)
    return o + jnp.dot(a.astype(BF16), vb, preferred_element_type=F32)


def _lower_bound(plb_ref):
    p = plb_ref[...]
    e = jnp.exp(p - jnp.max(p, axis=0, keepdims=True))
    return e[0:1, :] / jnp.sum(e, axis=0, keepdims=True)


def _hgrn_fwd_kernel(q_ref, z_ref, v_ref, plb_ref, o_ref, st_scr, *, nchunk):
    @pl.when(pl.program_id(2) == 0)
    def _():
        st_scr[...] = jnp.zeros_like(st_scr)

    lb = _lower_bound(plb_ref)
    C = HG_CHUNK
    for ci in range(nchunk):
        sl = slice(ci * C, (ci + 1) * C)
        o_ref[sl, :] = _hgrn_chunk(q_ref[sl, :], z_ref[sl, :], v_ref[sl, :], lb, st_scr, False)


def _hgrn_bwd_kernel(q_ref, z_ref, v_ref, plb_ref, of_ref, gh_ref, gn_ref, o_ref, st_scr, *, nchunk):
    @pl.when(pl.program_id(2) == 0)
    def _():
        st_scr[...] = jnp.zeros_like(st_scr)

    lb = _lower_bound(plb_ref)
    C = HG_CHUNK
    for ci in reversed(range(nchunk)):
        sl = slice(ci * C, (ci + 1) * C)
        o = _hgrn_chunk(q_ref[sl, :], z_ref[sl, :], v_ref[sl, :], lb, st_scr, True)
        o = o + of_ref[sl, :]
        o = o * lax.rsqrt(jnp.mean(o * o, axis=-1, keepdims=True) + EPS) * gn_ref[...]
        gh = gh_ref[sl, :]
        o_ref[sl, :] = (o * (gh * jax.nn.sigmoid(gh))).astype(o_ref.dtype)


def _hgrn(proj, p_lb_fwd, p_lb_bwd, g_hnorm):
    B, S, _ = proj.shape
    tb = 512
    nt = S // tb
    nchunk = tb // HG_CHUNK
    dk = HGRN_DK
    scratch = [pltpu.VMEM((dk, dk), F32)]

    def col_spec(col0, rev):
        if rev:
            return pl.BlockSpec((None, tb, dk), lambda b, h, t: (b, nt - 1 - t, col0 // dk + h))
        return pl.BlockSpec((None, tb, dk), lambda b, h, t: (b, t, col0 // dk + h))

    plb_spec = pl.BlockSpec((p_lb_fwd.shape[0], dk), lambda b, h, t: (0, h))
    sem = ("parallel", "parallel", "arbitrary")
    o_f = pl.pallas_call(
        functools.partial(_hgrn_fwd_kernel, nchunk=nchunk),
        grid=(B, HGRN_HEADS, nt),
        in_specs=[col_spec(COL_QH, False), col_spec(COL_ZF, False), col_spec(COL_IH, False), plb_spec],
        out_specs=pl.BlockSpec((None, tb, dk), lambda b, h, t: (b, t, h)),
        out_shape=jax.ShapeDtypeStruct((B, S, HG_V), F32),
        scratch_shapes=scratch,
        compiler_params=_cparams(sem),
    )(proj, proj, proj, p_lb_fwd)
    return pl.pallas_call(
        functools.partial(_hgrn_bwd_kernel, nchunk=nchunk),
        grid=(B, HGRN_HEADS, nt),
        in_specs=[col_spec(COL_QH, True), col_spec(COL_ZB, True), col_spec(COL_IH, True), plb_spec,
                  pl.BlockSpec((None, tb, dk), lambda b, h, t: (b, nt - 1 - t, h)),
                  col_spec(COL_GH, True),
                  pl.BlockSpec((1, dk), lambda b, h, t: (0, h))],
        out_specs=pl.BlockSpec((None, tb, dk), lambda b, h, t: (b, nt - 1 - t, h)),
        out_shape=jax.ShapeDtypeStruct((B, S, HG_V), BF16),
        scratch_shapes=scratch,
        compiler_params=_cparams(sem),
    )(proj, proj, proj, p_lb_bwd, o_f, proj, g_hnorm)


def _outproj_kernel(att_ref, hg_ref, ga0_ref, ga1_ref, gb0_ref, gb1_ref, x_ref, mod_ref, woa_ref, woh_ref,
                    wout_ref, g2_ref, wpq_ref, x1_ref, h2_ref, qp_ref):
    ya = jnp.dot(att_ref[...], woa_ref[...], preferred_element_type=F32)
    yh = jnp.dot(hg_ref[...], woh_ref[...], preferred_element_type=F32)
    ga = jnp.concatenate([ga0_ref[...], ga1_ref[...]], axis=-1)
    gb = jnp.concatenate([gb0_ref[...], gb1_ref[...]], axis=-1)
    merged = jax.nn.sigmoid(ga) * ya + jax.nn.sigmoid(gb) * yh
    x1 = x_ref[...] + mod_ref[2:3, :] * jnp.dot(merged.astype(BF16), wout_ref[...],
                                                preferred_element_type=F32)
    x1_ref[...] = x1
    y = x1 * lax.rsqrt(jnp.mean(x1 * x1, axis=-1, keepdims=True) + EPS) * g2_ref[...]
    h2 = y * (1.0 + mod_ref[4:5, :]) + mod_ref[3:4, :]
    h2_ref[...] = h2
    qp_ref[...] = jnp.dot(h2.astype(BF16), wpq_ref[...], preferred_element_type=F32)


def _outproj(att, hg, proj, x, mod3, woa, woh, wout, g2, wpq):
    B, S, _ = x.shape
    tm = 256
    half = D_MODEL // 2
    nq = wpq.shape[1]

    def tok(width, col):
        return pl.BlockSpec((None, tm, width), lambda b, i: (b, i, col))

    def full(shape):
        return pl.BlockSpec(shape, lambda b, i: (0, 0))

    return pl.pallas_call(
        _outproj_kernel,
        grid=(B, S // tm),
        in_specs=[
            tok(ATT_Q, 0), tok(HG_V, 0),
            tok(half, COL_GA // half), tok(half, COL_GA // half + 1),
            tok(half, COL_GB // half), tok(half, COL_GB // half + 1),
            tok(D_MODEL, 0),
            pl.BlockSpec((None, 6, D_MODEL), lambda b, i: (b, 0, 0)),
            full((ATT_Q, D_MODEL)), full((HG_V, D_MODEL)), full((D_MODEL, D_MODEL)),
            full((1, D_MODEL)), full((D_MODEL, nq)),
        ],
        out_specs=[tok(D_MODEL, 0), tok(D_MODEL, 0), tok(nq, 0)],
        out_shape=[jax.ShapeDtypeStruct((B, S, D_MODEL), F32),
                   jax.ShapeDtypeStruct((B, S, D_MODEL), F32),
                   jax.ShapeDtypeStruct((B, S, nq), F32)],
        compiler_params=_cparams(("parallel", "parallel")),
    )(att, hg, proj, proj, proj, proj, x, mod3, woa, woh, wout, g2, wpq)


def _route_kernel(q_ref, keys_ref, eid_ref, gate_ref, sv_scr, si_scr, cand_scr, cid_scr, ts_scr):
    tt = q_ref.shape[0]
    n = PEER_NKEYS
    k = PEER_TOPK
    q = q_ref[...].astype(BF16)
    iota_n = lax.broadcasted_iota(jnp.int32, (n, tt), 0).astype(F32)
    for p in range(2):
        s = _nt_dot(keys_ref[p * n:(p + 1) * n, :].astype(BF16), q[:, p * PEER_HALF:(p + 1) * PEER_HALF])
        for i in range(k):
            m = jnp.max(s, axis=0, keepdims=True)
            tagged = jnp.where(s == m, iota_n, float(n))
            idx = jnp.min(tagged, axis=0, keepdims=True)
            sv_scr[p, i:i + 1, :] = m
            si_scr[p, i:i + 1, :] = idx
            s = jnp.where(tagged == idx, -jnp.inf, s)
    ncand = cand_scr.shape[0]
    cand_scr[...] = jnp.full(cand_scr.shape, -jnp.inf, F32)
    cid_scr[...] = jnp.zeros(cid_scr.shape, F32)
    off = 0
    for a in range(k):
        nb = k // (a + 1)
        cand_scr[off:off + nb, :] = sv_scr[0, a:a + 1, :] + sv_scr[1, 0:nb, :]
        cid_scr[off:off + nb, :] = si_scr[0, a:a + 1, :] * float(n) + si_scr[1, 0:nb, :]
        off += nb
    c = cand_scr[...]
    ids = cid_scr[...]
    iota_c = lax.broadcasted_iota(jnp.int32, (ncand, tt), 0).astype(F32)
    for i in range(k):
        m = jnp.max(c, axis=0, keepdims=True)
        tagged = jnp.where(c == m, iota_c, float(ncand))
        sel = tagged == jnp.min(tagged, axis=0, keepdims=True)
        eid_ref[i:i + 1, :] = jnp.max(jnp.where(sel, ids, -1.0), axis=0, keepdims=True).astype(jnp.int32)
        ts_scr[i:i + 1, :] = m
        c = jnp.where(sel, -jnp.inf, c)
    ts = ts_scr[...]
    ex = jnp.exp(ts - ts[0:1, :])
    gate_ref[...] = ex / jnp.sum(ex, axis=0, keepdims=True)


def _route(qp, keys2d):
    n_tok = qp.shape[0]
    tt = 256
    k = PEER_TOPK
    qd = 2 * PEER_HALF
    npairs = sum(k // (a + 1) for a in range(k))
    ncand = -(-npairs // SUBLANES) * SUBLANES
    return pl.pallas_call(
        _route_kernel,
        grid=(n_tok // tt, PEER_HEADS),
        in_specs=[
            pl.BlockSpec((tt, qd), lambda i, h: (i, h)),
            pl.BlockSpec((2 * PEER_NKEYS, PEER_HALF), lambda i, h: (h, 0)),
        ],
        out_specs=[pl.BlockSpec((k, tt), lambda i, h: (h, i)),
                   pl.BlockSpec((k, tt), lambda i, h: (h, i))],
        out_shape=[jax.ShapeDtypeStruct((PEER_SEL, n_tok), jnp.int32),
                   jax.ShapeDtypeStruct((PEER_SEL, n_tok), F32)],
        scratch_shapes=[pltpu.VMEM((2, k, tt), F32), pltpu.VMEM((2, k, tt), F32),
                        pltpu.VMEM((ncand, tt), F32), pltpu.VMEM((ncand, tt), F32),
                        pltpu.VMEM((k, tt), F32)],
        compiler_params=_cparams(("parallel", "arbitrary")),
    )(qp, keys2d)


PEER_GROUP = SUBLANES
PEER_NGROUP = 4
PEER_TT = PEER_GROUP * PEER_NGROUP
U_MASK = 0xFFFF0000


PEER_CHUNKS = D_MODEL // LANES
PEER_SEL_TILES = PEER_SEL // SUBLANES


def _pack_tables(u, v):
    ub = lax.bitcast_convert_type(u.astype(BF16), jnp.uint16).astype(jnp.uint32)
    vb = lax.bitcast_convert_type(v.astype(BF16), jnp.uint16).astype(jnp.uint32)
    return ((ub << 16) | vb).reshape(u.shape[0], PEER_CHUNKS, 1, LANES)


def _peer_kernel(eid_ref, eidn_ref, gate_ref, h2_ref, x1_ref, mod_ref, gf_ref, tab_hbm, o_ref, wbuf, sem):
    i = pl.program_id(0)
    n = pl.num_programs(0)
    slot = i % 2
    G = PEER_GROUP

    def issue_rows(idx_ref, g, s, tok, j0, j1):
        for j in range(j0, j1):
            e = idx_ref[tok, j]
            dst = wbuf.at[s, tok, j // SUBLANES, pl.ds(0, PEER_CHUNKS), pl.ds(j % SUBLANES, 1)]
            pltpu.make_async_copy(tab_hbm.at[e], dst, sem.at[s, g]).start(priority=j % 2)

    def wait_group(g, s):
        rows = pl.ds(g * G, G)
        pltpu.make_async_copy(wbuf.at[s, rows], wbuf.at[s, rows], sem.at[s, g]).wait()

    per_step = PEER_SEL // (2 * PEER_CHUNKS)

    def compute_group(g, s):
        rows = pl.ds(pl.multiple_of(g * G, G), G)
        gate_t = gate_ref[rows, :].T
        h2g = h2_ref[rows, :]
        sub = lax.broadcasted_iota(jnp.int32, (G, D_MODEL), 0)
        y = jnp.zeros((G, D_MODEL), F32)
        for r in range(G):
            tok = g * G + r
            acc = jnp.zeros((PEER_SEL_TILES, SUBLANES, LANES), F32)
            for c in range(PEER_CHUNKS):
                issue_rows(eidn_ref, g, 1 - s, tok, c * per_step, (c + 1) * per_step)
                uf = lax.bitcast_convert_type(wbuf[s, tok, :, c] & jnp.uint32(U_MASK), F32)
                acc = acc + uf * h2g[r:r + 1, c * LANES:(c + 1) * LANES]
            act = jnp.sum(acc, axis=-1, keepdims=True)
            act = 0.5 * act * (1.0 + lax.erf(act * (1.0 / math.sqrt(2.0))))
            w = gate_t[:, r:r + 1].reshape(PEER_SEL_TILES, SUBLANES, 1) * act
            pieces = []
            for c in range(PEER_CHUNKS):
                issue_rows(eidn_ref, g, 1 - s, tok, (PEER_CHUNKS + c) * per_step,
                           (PEER_CHUNKS + c + 1) * per_step)
                vf = lax.bitcast_convert_type(wbuf[s, tok, :, c] << 16, F32)
                pieces.append(jnp.sum(jnp.sum(w * vf, axis=0), axis=0, keepdims=True))
            y = jnp.where(sub == r, jnp.concatenate(pieces, axis=-1), y)
        x2 = x1_ref[rows, :] + mod_ref[5:6, :] * y
        o_ref[rows, :] = x2 * lax.rsqrt(jnp.mean(x2 * x2, axis=-1, keepdims=True) + EPS) * gf_ref[...]

    @pl.when(i == 0)
    def _():
        def first(t, carry):
            issue_rows(eid_ref, t // G, 0, t, 0, PEER_SEL)
            return carry
        lax.fori_loop(0, PEER_TT, first, 0)

    def step(g, carry):
        wait_group(g, slot)
        compute_group(g, slot)
        return carry

    lax.fori_loop(0, PEER_NGROUP, step, 0)

    @pl.when(i == n - 1)
    def _():
        def drain(g, carry):
            wait_group(g, 1 - slot)
            return carry
        lax.fori_loop(0, PEER_NGROUP, drain, 0)


def _peer(eid_t, gate_t, h2, x1, mod3, g_final, table, seq):
    n_tok = h2.shape[0]
    tt = PEER_TT
    n = n_tok // tt
    smem_idx = lambda f: pl.BlockSpec((tt, PEER_SEL), f, memory_space=pltpu.SMEM)
    tok = lambda w: pl.BlockSpec((tt, w), lambda i: (i, 0))
    return pl.pallas_call(
        _peer_kernel,
        grid=(n,),
        in_specs=[
            smem_idx(lambda i: (i, 0)),
            smem_idx(lambda i: (jnp.minimum(i + 1, n - 1), 0)),
            tok(PEER_SEL), tok(D_MODEL), tok(D_MODEL),
            pl.BlockSpec((None, 6, D_MODEL), lambda i: ((i * tt) // seq, 0, 0)),
            pl.BlockSpec((1, D_MODEL), lambda i: (0, 0)),
            pl.BlockSpec(memory_space=pl.ANY),
        ],
        out_specs=tok(D_MODEL),
        out_shape=jax.ShapeDtypeStruct((n_tok, D_MODEL), F32),
        scratch_shapes=[pltpu.VMEM((2, tt, PEER_SEL_TILES, PEER_CHUNKS, SUBLANES, LANES), jnp.uint32),
                        pltpu.SemaphoreType.DMA((2, PEER_NGROUP))],
        compiler_params=_cparams(("arbitrary",)),
    )(eid_t, eid_t, gate_t, h2, x1, mod3, g_final, table)


def _run_group(x, c, w_mod, b_mod, g1, w_in_bf, sink, p_lb_fwd, p_lb_bwd, g_hnorm, woa, woh, wout, g2,
               wpq, keys2d, table, g_final):
    B, S, _ = x.shape
    mod3 = _modulation(c, w_mod, b_mod).reshape(B, 6, D_MODEL)
    proj = _inproj(x, mod3, g1, w_in_bf)
    att = _attention(proj, sink)
    hg = _hgrn(proj, p_lb_fwd, p_lb_bwd, g_hnorm)
    x1, h2, qp = _outproj(att, hg, proj, x, mod3, woa, woh, wout, g2, wpq)
    n_tok = B * S
    eid, gate = _route(qp.reshape(n_tok, -1), keys2d)
    y = _peer(eid.T, gate.T, h2.reshape(n_tok, D_MODEL), x1.reshape(n_tok, D_MODEL), mod3, g_final, table, S)
    return y.reshape(B, S, D_MODEL)


def kernel(x_prompt, x_sample, c_prompt, c_sample, w_mod, b_mod, g_norm1, w_in, att_sink, p_lb_fwd, p_lb_bwd,
           g_hnorm, w_o_att, w_o_hgrn, w_out, g_norm2, w_pq, peer_keys, peer_u, peer_v, g_final):
    assert w_mod.shape[0] == 1 and p_lb_fwd.shape[0] == 2, "single-layer encoder"
    shared = (
        w_mod[0], b_mod[0], g_norm1[0].reshape(1, D_MODEL), w_in[0].astype(BF16), att_sink[0],
        p_lb_fwd, p_lb_bwd, g_hnorm[0].reshape(1, HG_V),
        w_o_att[0].astype(BF16), w_o_hgrn[0].astype(BF16), w_out[0].astype(BF16),
        g_norm2[0].reshape(1, D_MODEL), w_pq[0].astype(BF16),
        peer_keys[0].reshape(PEER_HEADS * 2 * PEER_NKEYS, PEER_HALF),
        _pack_tables(peer_u[0], peer_v[0]), g_final.reshape(1, D_MODEL),
    )
    return (_run_group(x_prompt, c_prompt, *shared), _run_group(x_sample, c_sample, *shared))
```

```python
import functools
import math

import jax
import jax.numpy as jnp
from jax import lax
from jax.experimental import pallas as pl
from jax.experimental.pallas import tpu as pltpu

F32 = jnp.float32
BF16 = jnp.bfloat16
HIGHEST = lax.Precision.HIGHEST

D_MODEL = 1024
ATT_HEADS = 16
ATT_KV_HEADS = 4
ATT_HEAD_DIM = 64
ATT_GROUP = ATT_HEADS // ATT_KV_HEADS
WINDOW = 128
ATT_Q = ATT_HEADS * ATT_HEAD_DIM
ATT_KV = ATT_KV_HEADS * ATT_HEAD_DIM
HGRN_HEADS = 8
HGRN_DK = 128
HG_K = HGRN_HEADS * HGRN_DK
HG_V = HG_K
IN_COLS = ATT_Q + 2 * ATT_KV + 3 * HG_K + 2 * HG_V + 2 * D_MODEL
PEER_HEADS = 8
PEER_NKEYS = 128
PEER_HALF = 128
PEER_TOPK = 16
PEER_SEL = PEER_HEADS * PEER_TOPK
EPS = 1e-6

COL_QA = 0
COL_KA = COL_QA + ATT_Q
COL_VA = COL_KA + ATT_KV
COL_QH = COL_VA + ATT_KV
COL_ZF = COL_QH + HG_K
COL_ZB = COL_ZF + HG_K
COL_IH = COL_ZB + HG_K
COL_GH = COL_IH + HG_V
COL_GA = COL_GH + HG_V
COL_GB = COL_GA + D_MODEL

LANES = 128
SUBLANES = 8
VMEM_LIMIT = 56 * 1024 * 1024

HG_CHUNK = 128
HG_SUB = SUBLANES
HG_LEVELS = (64, 32, 16, 8)


def _cparams(sem):
    return pltpu.CompilerParams(dimension_semantics=sem, vmem_limit_bytes=VMEM_LIMIT)


def _nt_dot(a, b):
    return lax.dot_general(a, b, (((1,), (1,)), ((), ())), preferred_element_type=F32)


def _tn_dot(a, b):
    return lax.dot_general(a, b, (((0,), (0,)), ((), ())), preferred_element_type=F32)


def _mod_kernel(c_ref, w_ref, b_ref, o_ref):
    c = c_ref[...]
    sc = c * jax.nn.sigmoid(c)
    o_ref[...] = jnp.dot(sc, w_ref[...], preferred_element_type=F32, precision=HIGHEST) + b_ref[...]


def _modulation(c, w_mod, b_mod):
    B = c.shape[0]
    n = w_mod.shape[1]
    tn = D_MODEL
    return pl.pallas_call(
        _mod_kernel,
        grid=(n // tn,),
        in_specs=[
            pl.BlockSpec((B, D_MODEL), lambda j: (0, 0)),
            pl.BlockSpec((D_MODEL, tn), lambda j: (0, j)),
            pl.BlockSpec((1, tn), lambda j: (0, j)),
        ],
        out_specs=pl.BlockSpec((B, tn), lambda j: (0, j)),
        out_shape=jax.ShapeDtypeStruct((B, n), F32),
        compiler_params=_cparams(("parallel",)),
    )(c, w_mod, b_mod.reshape(1, n))


def _inproj_kernel(x_ref, mod_ref, g_ref, w_ref, o_ref, h_scr):
    @pl.when(pl.program_id(2) == 0)
    def _():
        x = x_ref[...]
        y = x * lax.rsqrt(jnp.mean(x * x, axis=-1, keepdims=True) + EPS) * g_ref[...]
        h = y * (1.0 + mod_ref[1:2, :]) + mod_ref[0:1, :]
        h_scr[...] = h.astype(BF16)

    o_ref[...] = jnp.dot(h_scr[...], w_ref[...], preferred_element_type=F32)


def _inproj(x, mod3, g1, w_in_bf):
    B, S, _ = x.shape
    tm = 512
    tn = 2176
    return pl.pallas_call(
        _inproj_kernel,
        grid=(B, S // tm, IN_COLS // tn),
        in_specs=[
            pl.BlockSpec((None, tm, D_MODEL), lambda b, i, j: (b, i, 0)),
            pl.BlockSpec((None, 6, D_MODEL), lambda b, i, j: (b, 0, 0)),
            pl.BlockSpec((1, D_MODEL), lambda b, i, j: (0, 0)),
            pl.BlockSpec((D_MODEL, tn), lambda b, i, j: (0, j)),
        ],
        out_specs=pl.BlockSpec((None, tm, tn), lambda b, i, j: (b, i, j)),
        out_shape=jax.ShapeDtypeStruct((B, S, IN_COLS), F32),
        scratch_shapes=[pltpu.VMEM((tm, D_MODEL), BF16)],
        compiler_params=_cparams(("parallel", "parallel", "arbitrary")),
    )(x, mod3, g1, w_in_bf)


def _attn_kernel(sink_ref, q_ref, kp_ref, kc_ref, kn_ref, vp_ref, vc_ref, vn_ref, o_ref, *, nblk):
    j = pl.program_id(1)
    blk = WINDOW
    span = 3 * blk
    q = q_ref[...].astype(BF16)
    k = jnp.concatenate([kp_ref[...], kc_ref[...], kn_ref[...]], axis=0).astype(BF16)
    v = jnp.concatenate([vp_ref[...], vc_ref[...], vn_ref[...]], axis=0).astype(BF16)
    qi = lax.broadcasted_iota(jnp.int32, (blk, span), 0)
    kj = lax.broadcasted_iota(jnp.int32, (blk, span), 1)
    dist = jnp.abs(kj - blk - qi)
    lo = jnp.where(j == 0, blk, 0)
    hi = jnp.where(j == nblk - 1, 2 * blk, span)
    valid = (dist <= WINDOW) & (kj >= lo) & (kj < hi)
    distf = dist.astype(F32)
    scale = 1.0 / math.sqrt(ATT_HEAD_DIM)
    outs = []
    for hh in range(ATT_HEADS):
        h = hh // ATT_GROUP
        slope = 2.0 ** (-8.0 * (hh + 1) / ATT_HEADS)
        qh = q[:, hh * ATT_HEAD_DIM:(hh + 1) * ATT_HEAD_DIM]
        kh = k[:, h * ATT_HEAD_DIM:(h + 1) * ATT_HEAD_DIM]
        vh = v[:, h * ATT_HEAD_DIM:(h + 1) * ATT_HEAD_DIM]
        s = _nt_dot(qh, kh) * scale - slope * distf
        s = jnp.where(valid, s, -jnp.inf)
        sink = sink_ref[hh]
        m = jnp.maximum(jnp.max(s, axis=-1, keepdims=True), sink)
        p = jnp.exp(s - m)
        denom = jnp.sum(p, axis=-1, keepdims=True) + jnp.exp(sink - m)
        o = jnp.dot(p.astype(BF16), vh, preferred_element_type=F32) / denom
        outs.append(o)
    o_ref[...] = jnp.concatenate(outs, axis=-1).astype(o_ref.dtype)


def _attention(proj, sink):
    B, S, _ = proj.shape
    blk = WINDOW
    nblk = S // blk
    kcol = COL_KA // ATT_KV
    vcol = COL_VA // ATT_KV

    def kv_spec(col, off):
        return pl.BlockSpec(
            (None, blk, ATT_KV),
            lambda b, j: (b, jnp.clip(j + off, 0, nblk - 1), col))

    return pl.pallas_call(
        functools.partial(_attn_kernel, nblk=nblk),
        grid=(B, nblk),
        in_specs=[
            pl.BlockSpec(memory_space=pltpu.SMEM),
            pl.BlockSpec((None, blk, ATT_Q), lambda b, j: (b, j, 0)),
            kv_spec(kcol, -1), kv_spec(kcol, 0), kv_spec(kcol, 1),
            kv_spec(vcol, -1), kv_spec(vcol, 0), kv_spec(vcol, 1),
        ],
        out_specs=pl.BlockSpec((None, blk, ATT_Q), lambda b, j: (b, j, 0)),
        out_shape=jax.ShapeDtypeStruct((B, S, ATT_Q), BF16),
        compiler_params=_cparams(("parallel", "parallel")),
    )(sink, proj, proj, proj, proj, proj, proj, proj)


def _hgrn_chunk(q_raw, z, v, lb, st_scr, reverse):
    C = HG_CHUNK
    q = q_raw * jax.nn.sigmoid(q_raw)
    f = lb + (1.0 - lb) * jax.nn.sigmoid(z)
    kk = (1.0 - lb) * jax.nn.sigmoid(-z)
    logf = jnp.log(f)
    row = lax.broadcasted_iota(jnp.int32, (C, C), 0)
    col = lax.broadcasted_iota(jnp.int32, (C, C), 1)
    tri = jnp.where((row <= col) if reverse else (row >= col), 1.0, 0.0).astype(F32)
    b = jnp.dot(tri, logf, preferred_element_type=F32, precision=HIGHEST)
    edge = b[0:1, :] if reverse else b[C - 1:C, :]
    vb = v.astype(BF16)

    st = st_scr[...]
    o = _nt_dot((q * jnp.exp(b)).astype(BF16), st.astype(BF16))
    kh = (kk * jnp.exp(edge - b)).astype(BF16)
    st_scr[...] = st * jnp.exp(edge) + _tn_dot(vb, kh)

    pos = lax.broadcasted_iota(jnp.int32, (C, 1), 0)
    a = jnp.zeros((C, C), F32)
    for hsz in HG_LEVELS:
        seg = 2 * hsz
        refs = []
        for s0 in range(0, C, seg):
            r = s0 + hsz if reverse else s0 + hsz - 1
            refs.append(jnp.broadcast_to(b[r:r + 1, :], (seg, HGRN_DK)))
        bmid = jnp.concatenate(refs, axis=0) if len(refs) > 1 else refs[0]
        e = jnp.exp(-jnp.abs(b - bmid))
        upper = (pos % seg) >= hsz
        qrole = jnp.logical_not(upper) if reverse else upper
        ql = jnp.where(qrole, q * e, 0.0).astype(BF16)
        kl = jnp.where(qrole, 0.0, kk * e).astype(BF16)
        al = _nt_dot(ql, kl)
        same = (row // seg) == (col // seg)
        rq = (row % seg) >= hsz
        ck = (col % seg) >= hsz
        if reverse:
            m = same & jnp.logical_not(rq) & ck
        else:
            m = same & rq & jnp.logical_not(ck)
        a = jnp.where(m, al, a)
    nb = C // HG_SUB
    shape3 = (nb, HG_SUB, HGRN_DK)
    b3 = b.reshape(shape3)
    q3 = q.reshape(shape3)
    k3 = kk.reshape(shape3)
    sub = lax.broadcasted_iota(jnp.int32, shape3, 1)
    block_col0 = (row // HG_SUB) * HG_SUB
    for s in range(HG_SUB):
        arg = jnp.minimum(b3 - b3[:, s:s + 1, :], 0.0)
        keep = (sub <= s) if reverse else (sub >= s)
        g = jnp.where(keep, q3 * k3[:, s:s + 1, :] * jnp.exp(arg), 0.0)
        w = jnp.sum(g, axis=-1, keepdims=True).reshape(C, 1)
        a = jnp.where(col == block_col0 + s, w, a)
    return o + jnp.dot(a.astype(BF16), vb, preferred_element_type=F32)


def _lower_bound(plb_ref):
    p = plb_ref[...]
    e = jnp.exp(p - jnp.max(p, axis=0, keepdims=True))
    return e[0:1, :] / jnp.sum(e, axis=0, keepdims=True)


def _hgrn_fwd_kernel(q_ref, z_ref, v_ref, plb_ref, o_ref, st_scr, *, nchunk):
    @pl.when(pl.program_id(2) == 0)
    def _():
        st_scr[...] = jnp.zeros_like(st_scr)

    lb = _lower_bound(plb_ref)
    C = HG_CHUNK
    for ci in range(nchunk):
        sl = slice(ci * C, (ci + 1) * C)
        o_ref[sl, :] = _hgrn_chunk(q_ref[sl, :], z_ref[sl, :], v_ref[sl, :], lb, st_scr, False)


def _hgrn_bwd_kernel(q_ref, z_ref, v_ref, plb_ref, of_ref, gh_ref, gn_ref, o_ref, st_scr, *, nchunk):
    @pl.when(pl.program_id(2) == 0)
    def _():
        st_scr[...] = jnp.zeros_like(st_scr)

    lb = _lower_bound(plb_ref)
    C = HG_CHUNK
    for ci in reversed(range(nchunk)):
        sl = slice(ci * C, (ci + 1) * C)
        o = _hgrn_chunk(q_ref[sl, :], z_ref[sl, :], v_ref[sl, :], lb, st_scr, True)
        o = o + of_ref[sl, :]
        o = o * lax.rsqrt(jnp.mean(o * o, axis=-1, keepdims=True) + EPS) * gn_ref[...]
        gh = gh_ref[sl, :]
        o_ref[sl, :] = (o * (gh * jax.nn.sigmoid(gh))).astype(o_ref.dtype)


def _hgrn(proj, p_lb_fwd, p_lb_bwd, g_hnorm):
    B, S, _ = proj.shape
    tb = min(S, 1024)
    nt = S // tb
    nchunk = tb // HG_CHUNK
    dk = HGRN_DK
    scratch = [pltpu.VMEM((dk, dk), F32)]

    def col_spec(col0, rev):
        if rev:
            return pl.BlockSpec((None, tb, dk), lambda b, h, t: (b, nt - 1 - t, col0 // dk + h))
        return pl.BlockSpec((None, tb, dk), lambda b, h, t: (b, t, col0 // dk + h))

    plb_spec = pl.BlockSpec((p_lb_fwd.shape[0], dk), lambda b, h, t: (0, h))
    sem = ("parallel", "parallel", "arbitrary")
    o_f = pl.pallas_call(
        functools.partial(_hgrn_fwd_kernel, nchunk=nchunk),
        grid=(B, HGRN_HEADS, nt),
        in_specs=[col_spec(COL_QH, False), col_spec(COL_ZF, False), col_spec(COL_IH, False), plb_spec],
        out_specs=pl.BlockSpec((None, tb, dk), lambda b, h, t: (b, t, h)),
        out_shape=jax.ShapeDtypeStruct((B, S, HG_V), F32),
        scratch_shapes=scratch,
        compiler_params=_cparams(sem),
    )(proj, proj, proj, p_lb_fwd)
    return pl.pallas_call(
        functools.partial(_hgrn_bwd_kernel, nchunk=nchunk),
        grid=(B, HGRN_HEADS, nt),
        in_specs=[col_spec(COL_QH, True), col_spec(COL_ZB, True), col_spec(COL_IH, True), plb_spec,
                  pl.BlockSpec((None, tb, dk), lambda b, h, t: (b, nt - 1 - t, h)),
                  col_spec(COL_GH, True),
                  pl.BlockSpec((1, dk), lambda b, h, t: (0, h))],
        out_specs=pl.BlockSpec((None, tb, dk), lambda b, h, t: (b, nt - 1 - t, h)),
        out_shape=jax.ShapeDtypeStruct((B, S, HG_V), BF16),
        scratch_shapes=scratch,
        compiler_params=_cparams(sem),
    )(proj, proj, proj, p_lb_bwd, o_f, proj, g_hnorm)


def _outproj_kernel(att_ref, hg_ref, ga0_ref, ga1_ref, gb0_ref, gb1_ref, x_ref, mod_ref, woa_ref, woh_ref,
                    wout_ref, g2_ref, wpq_ref, x1_ref, h2_ref, qp_ref):
    ya = jnp.dot(att_ref[...], woa_ref[...], preferred_element_type=F32)
    yh = jnp.dot(hg_ref[...], woh_ref[...], preferred_element_type=F32)
    ga = jnp.concatenate([ga0_ref[...], ga1_ref[...]], axis=-1)
    gb = jnp.concatenate([gb0_ref[...], gb1_ref[...]], axis=-1)
    merged = jax.nn.sigmoid(ga) * ya + jax.nn.sigmoid(gb) * yh
    x1 = x_ref[...] + mod_ref[2:3, :] * jnp.dot(merged.astype(BF16), wout_ref[...],
                                                preferred_element_type=F32)
    x1_ref[...] = x1
    y = x1 * lax.rsqrt(jnp.mean(x1 * x1, axis=-1, keepdims=True) + EPS) * g2_ref[...]
    h2 = y * (1.0 + mod_ref[4:5, :]) + mod_ref[3:4, :]
    h2_ref[...] = h2
    qp_ref[...] = jnp.dot(h2.astype(BF16), wpq_ref[...], preferred_element_type=F32)


def _outproj(att, hg, proj, x, mod3, woa, woh, wout, g2, wpq):
    B, S, _ = x.shape
    tm = 256
    half = D_MODEL // 2
    nq = wpq.shape[1]

    def tok(width, col):
        return pl.BlockSpec((None, tm, width), lambda b, i: (b, i, col))

    def full(shape):
        return pl.BlockSpec(shape, lambda b, i: (0, 0))

    return pl.pallas_call(
        _outproj_kernel,
        grid=(B, S // tm),
        in_specs=[
            tok(ATT_Q, 0), tok(HG_V, 0),
            tok(half, COL_GA // half), tok(half, COL_GA // half + 1),
            tok(half, COL_GB // half), tok(half, COL_GB // half + 1),
            tok(D_MODEL, 0),
            pl.BlockSpec((None, 6, D_MODEL), lambda b, i: (b, 0, 0)),
            full((ATT_Q, D_MODEL)), full((HG_V, D_MODEL)), full((D_MODEL, D_MODEL)),
            full((1, D_MODEL)), full((D_MODEL, nq)),
        ],
        out_specs=[tok(D_MODEL, 0), tok(D_MODEL, 0), tok(nq, 0)],
        out_shape=[jax.ShapeDtypeStruct((B, S, D_MODEL), F32),
                   jax.ShapeDtypeStruct((B, S, D_MODEL), F32),
                   jax.ShapeDtypeStruct((B, S, nq), F32)],
        compiler_params=_cparams(("parallel", "parallel")),
    )(att, hg, proj, proj, proj, proj, x, mod3, woa, woh, wout, g2, wpq)


def _route_kernel(q_ref, keys_ref, eid_ref, gate_ref, sv_scr, si_scr, cand_scr, cid_scr, ts_scr):
    tt = q_ref.shape[0]
    n = PEER_NKEYS
    k = PEER_TOPK
    q = q_ref[...].astype(BF16)
    iota_n = lax.broadcasted_iota(jnp.int32, (n, tt), 0).astype(F32)
    for p in range(2):
        s = _nt_dot(keys_ref[p * n:(p + 1) * n, :].astype(BF16), q[:, p * PEER_HALF:(p + 1) * PEER_HALF])
        for i in range(k):
            m = jnp.max(s, axis=0, keepdims=True)
            tagged = jnp.where(s == m, iota_n, float(n))
            idx = jnp.min(tagged, axis=0, keepdims=True)
            sv_scr[p, i:i + 1, :] = m
            si_scr[p, i:i + 1, :] = idx
            s = jnp.where(tagged == idx, -jnp.inf, s)
    ncand = cand_scr.shape[0]
    cand_scr[...] = jnp.full(cand_scr.shape, -jnp.inf, F32)
    cid_scr[...] = jnp.zeros(cid_scr.shape, F32)
    off = 0
    for a in range(k):
        nb = k // (a + 1)
        cand_scr[off:off + nb, :] = sv_scr[0, a:a + 1, :] + sv_scr[1, 0:nb, :]
        cid_scr[off:off + nb, :] = si_scr[0, a:a + 1, :] * float(n) + si_scr[1, 0:nb, :]
        off += nb
    c = cand_scr[...]
    ids = cid_scr[...]
    iota_c = lax.broadcasted_iota(jnp.int32, (ncand, tt), 0).astype(F32)
    for i in range(k):
        m = jnp.max(c, axis=0, keepdims=True)
        tagged = jnp.where(c == m, iota_c, float(ncand))
        sel = tagged == jnp.min(tagged, axis=0, keepdims=True)
        eid_ref[i:i + 1, :] = jnp.max(jnp.where(sel, ids, -1.0), axis=0, keepdims=True).astype(jnp.int32)
        ts_scr[i:i + 1, :] = m
        c = jnp.where(sel, -jnp.inf, c)
    ts = ts_scr[...]
    ex = jnp.exp(ts - ts[0:1, :])
    gate_ref[...] = ex / jnp.sum(ex, axis=0, keepdims=True)


def _route(qp, keys2d):
    n_tok = qp.shape[0]
    tt = 256
    k = PEER_TOPK
    qd = 2 * PEER_HALF
    npairs = sum(k // (a + 1) for a in range(k))
    ncand = -(-npairs // SUBLANES) * SUBLANES
    return pl.pallas_call(
        _route_kernel,
        grid=(n_tok // tt, PEER_HEADS),
        in_specs=[
            pl.BlockSpec((tt, qd), lambda i, h: (i, h)),
            pl.BlockSpec((2 * PEER_NKEYS, PEER_HALF), lambda i, h: (h, 0)),
        ],
        out_specs=[pl.BlockSpec((k, tt), lambda i, h: (h, i)),
                   pl.BlockSpec((k, tt), lambda i, h: (h, i))],
        out_shape=[jax.ShapeDtypeStruct((PEER_SEL, n_tok), jnp.int32),
                   jax.ShapeDtypeStruct((PEER_SEL, n_tok), F32)],
        scratch_shapes=[pltpu.VMEM((2, k, tt), F32), pltpu.VMEM((2, k, tt), F32),
                        pltpu.VMEM((ncand, tt), F32), pltpu.VMEM((ncand, tt), F32),
                        pltpu.VMEM((k, tt), F32)],
        compiler_params=_cparams(("parallel", "arbitrary")),
    )(qp, keys2d)


PEER_GROUP = SUBLANES
PEER_NGROUP = 4
PEER_TT = PEER_GROUP * PEER_NGROUP
U_MASK = 0xFFFF0000


PEER_CHUNKS = D_MODEL // LANES
PEER_SEL_TILES = PEER_SEL // SUBLANES


def _pack_tables(u, v):
    ub = lax.bitcast_convert_type(u.astype(BF16), jnp.uint16).astype(jnp.uint32)
    vb = lax.bitcast_convert_type(v.astype(BF16), jnp.uint16).astype(jnp.uint32)
    return ((ub << 16) | vb).reshape(u.shape[0], PEER_CHUNKS, LANES)


def _peer_kernel(eid_ref, eidn_ref, gate_ref, h2_ref, x1_ref, mod_ref, gf_ref, tab_hbm, o_ref, wbuf, sem, w_scr):
    i = pl.program_id(0)
    n = pl.num_programs(0)
    slot = i % 2
    G = PEER_GROUP

    def issue_rows(idx_ref, g, s, tok, j0, j1):
        for j in range(j0, j1):
            e = idx_ref[tok, j]
            pltpu.make_async_copy(tab_hbm.at[e], wbuf.at[s, tok, j], sem.at[s, g]).start(priority=j % 2)

    def wait_group(g, s):
        rows = pl.ds(g * G, G)
        pltpu.make_async_copy(wbuf.at[s, rows], wbuf.at[s, rows], sem.at[s, g]).wait()

    per_step = PEER_SEL // (2 * PEER_SEL_TILES)
    sub8 = lax.broadcasted_iota(jnp.int32, (SUBLANES, LANES), 0)

    def fold(x, y, h):
        low = (sub8 % (2 * h)) < h
        if 2 * h == SUBLANES:
            return jnp.where(low, x, y) + pltpu.roll(jnp.where(low, y, x), h, 0)
        return jnp.where(low, x + pltpu.roll(x, SUBLANES - h, 0), y + pltpu.roll(y, h, 0))

    fold_order = (0, 4, 2, 6, 1, 5, 3, 7)

    def chunk_sums(tiles):
        t = [tiles[k] for k in fold_order]
        t = [fold(t[0], t[1], 4), fold(t[2], t[3], 4), fold(t[4], t[5], 4), fold(t[6], t[7], 4)]
        t = [fold(t[0], t[1], 2), fold(t[2], t[3], 2)]
        return fold(t[0], t[1], 1)

    def compute_group(g, s):
        rows = pl.ds(pl.multiple_of(g * G, G), G)
        gate_t = gate_ref[rows, :].T
        h2g = h2_ref[rows, :]
        sub = lax.broadcasted_iota(jnp.int32, (G, D_MODEL), 0)
        y = jnp.zeros((G, D_MODEL), F32)
        for r in range(G):
            tok = g * G + r
            h2t = jnp.zeros((SUBLANES, LANES), F32)
            for c in range(PEER_CHUNKS):
                h2t = jnp.where(sub8 == c, h2g[r:r + 1, c * LANES:(c + 1) * LANES], h2t)
            part = []
            for a in range(PEER_SEL_TILES):
                issue_rows(eidn_ref, g, 1 - s, tok, a * per_step, (a + 1) * per_step)
                part.append(chunk_sums([
                    lax.bitcast_convert_type(wbuf[s, tok, a * SUBLANES + b] & jnp.uint32(U_MASK), F32) * h2t
                    for b in range(SUBLANES)]))
            act = jnp.sum(jnp.stack(part), axis=-1, keepdims=True)
            act = 0.5 * act * (1.0 + lax.erf(act * (1.0 / math.sqrt(2.0))))
            w = gate_t[:, r:r + 1].reshape(PEER_SEL_TILES, SUBLANES, 1) * act
            w_scr[...] = jnp.broadcast_to(w, (PEER_SEL_TILES, SUBLANES, LANES))
            acc = jnp.zeros((SUBLANES, LANES), F32)
            for a in range(PEER_SEL_TILES):
                issue_rows(eidn_ref, g, 1 - s, tok, (PEER_SEL_TILES + a) * per_step,
                           (PEER_SEL_TILES + a + 1) * per_step)
                for b in range(SUBLANES):
                    vf = lax.bitcast_convert_type(wbuf[s, tok, a * SUBLANES + b] << 16, F32)
                    acc = acc + w_scr[a, b:b + 1, :] * vf
            yrow = jnp.concatenate([acc[c:c + 1, :] for c in range(PEER_CHUNKS)], axis=-1)
            y = jnp.where(sub == r, yrow, y)
        x2 = x1_ref[rows, :] + mod_ref[5:6, :] * y
        o_ref[rows, :] = x2 * lax.rsqrt(jnp.mean(x2 * x2, axis=-1, keepdims=True) + EPS) * gf_ref[...]

    @pl.when(i == 0)
    def _():
        def first(t, carry):
            issue_rows(eid_ref, t // G, 0, t, 0, PEER_SEL)
            return carry
        lax.fori_loop(0, PEER_TT, first, 0)

    def step(g, carry):
        wait_group(g, slot)
        compute_group(g, slot)
        return carry

    lax.fori_loop(0, PEER_NGROUP, step, 0)

    @pl.when(i == n - 1)
    def _():
        def drain(g, carry):
            wait_group(g, 1 - slot)
            return carry
        lax.fori_loop(0, PEER_NGROUP, drain, 0)


def _peer(eid_t, gate_t, h2, x1, mod3, g_final, table, seq):
    n_tok = h2.shape[0]
    tt = PEER_TT
    n = n_tok // tt
    smem_idx = lambda f: pl.BlockSpec((tt, PEER_SEL), f, memory_space=pltpu.SMEM)
    tok = lambda w: pl.BlockSpec((tt, w), lambda i: (i, 0))
    return pl.pallas_call(
        _peer_kernel,
        grid=(n,),
        in_specs=[
            smem_idx(lambda i: (i, 0)),
            smem_idx(lambda i: (jnp.minimum(i + 1, n - 1), 0)),
            tok(PEER_SEL), tok(D_MODEL), tok(D_MODEL),
            pl.BlockSpec((None, 6, D_MODEL), lambda i: ((i * tt) // seq, 0, 0)),
            pl.BlockSpec((1, D_MODEL), lambda i: (0, 0)),
            pl.BlockSpec(memory_space=pl.ANY),
        ],
        out_specs=tok(D_MODEL),
        out_shape=jax.ShapeDtypeStruct((n_tok, D_MODEL), F32),
        scratch_shapes=[pltpu.VMEM((2, tt, PEER_SEL, PEER_CHUNKS, LANES), jnp.uint32),
                        pltpu.SemaphoreType.DMA((2, PEER_NGROUP)),
                        pltpu.VMEM((PEER_SEL_TILES, SUBLANES, LANES), F32)],
        compiler_params=_cparams(("arbitrary",)),
    )(eid_t, eid_t, gate_t, h2, x1, mod3, g_final, table)


def _run_group(x, c, w_mod, b_mod, g1, w_in_bf, sink, p_lb_fwd, p_lb_bwd, g_hnorm, woa, woh, wout, g2,
               wpq, keys2d, table, g_final):
    B, S, _ = x.shape
    mod3 = _modulation(c, w_mod, b_mod).reshape(B, 6, D_MODEL)
    proj = _inproj(x, mod3, g1, w_in_bf)
    att = _attention(proj, sink)
    hg = _hgrn(proj, p_lb_fwd, p_lb_bwd, g_hnorm)
    x1, h2, qp = _outproj(att, hg, proj, x, mod3, woa, woh, wout, g2, wpq)
    n_tok = B * S
    eid, gate = _route(qp.reshape(n_tok, -1), keys2d)
    y = _peer(eid.T, gate.T, h2.reshape(n_tok, D_MODEL), x1.reshape(n_tok, D_MODEL), mod3, g_final, table, S)
    return y.reshape(B, S, D_MODEL)


def kernel(x_prompt, x_sample, c_prompt, c_sample, w_mod, b_mod, g_norm1, w_in, att_sink, p_lb_fwd, p_lb_bwd,
           g_hnorm, w_o_att, w_o_hgrn, w_out, g_norm2, w_pq, peer_keys, peer_u, peer_v, g_final):
    assert w_mod.shape[0] == 1 and p_lb_fwd.shape[0] == 2, "single-layer encoder"
    shared = (
        w_mod[0], b_mod[0], g_norm1[0].reshape(1, D_MODEL), w_in[0].astype(BF16), att_sink[0],
        p_lb_fwd, p_lb_bwd, g_hnorm[0].reshape(1, HG_V),
        w_o_att[0].astype(BF16), w_o_hgrn[0].astype(BF16), w_out[0].astype(BF16),
        g_norm2[0].reshape(1, D_MODEL), w_pq[0].astype(BF16),
        peer_keys[0].reshape(PEER_HEADS * 2 * PEER_NKEYS, PEER_HALF),
        _pack_tables(peer_u[0], peer_v[0]), g_final.reshape(1, D_MODEL),
    )
    return (_run_group(x_prompt, c_prompt, *shared), _run_group(x_sample, c_sample, *shared))
```

```python
import functools
import math

import jax
import jax.numpy as jnp
from jax import lax
from jax.experimental import pallas as pl
from jax.experimental.pallas import tpu as pltpu

F32 = jnp.float32
BF16 = jnp.bfloat16
HIGHEST = lax.Precision.HIGHEST

D_MODEL = 1024
ATT_HEADS = 16
ATT_KV_HEADS = 4
ATT_HEAD_DIM = 64
ATT_GROUP = ATT_HEADS // ATT_KV_HEADS
WINDOW = 128
ATT_Q = ATT_HEADS * ATT_HEAD_DIM
ATT_KV = ATT_KV_HEADS * ATT_HEAD_DIM
HGRN_HEADS = 8
HGRN_DK = 128
HG_K = HGRN_HEADS * HGRN_DK
HG_V = HG_K
IN_COLS = ATT_Q + 2 * ATT_KV + 3 * HG_K + 2 * HG_V + 2 * D_MODEL
PEER_HEADS = 8
PEER_NKEYS = 128
PEER_HALF = 128
PEER_TOPK = 16
PEER_SEL = PEER_HEADS * PEER_TOPK
EPS = 1e-6

COL_QA = 0
COL_KA = COL_QA + ATT_Q
COL_VA = COL_KA + ATT_KV
COL_QH = COL_VA + ATT_KV
COL_ZF = COL_QH + HG_K
COL_ZB = COL_ZF + HG_K
COL_IH = COL_ZB + HG_K
COL_GH = COL_IH + HG_V
COL_GA = COL_GH + HG_V
COL_GB = COL_GA + D_MODEL

LANES = 128
SUBLANES = 8
VMEM_LIMIT = 56 * 1024 * 1024

HG_CHUNK = 128
HG_SUB = SUBLANES
HG_LEVELS = (64, 32, 16, 8)


def _cparams(sem):
    return pltpu.CompilerParams(dimension_semantics=sem, vmem_limit_bytes=VMEM_LIMIT)


def _nt_dot(a, b):
    return lax.dot_general(a, b, (((1,), (1,)), ((), ())), preferred_element_type=F32)


def _tn_dot(a, b):
    return lax.dot_general(a, b, (((0,), (0,)), ((), ())), preferred_element_type=F32)


def _mod_kernel(c_ref, w_ref, b_ref, o_ref):
    c = c_ref[...]
    sc = c * jax.nn.sigmoid(c)
    o_ref[...] = jnp.dot(sc, w_ref[...], preferred_element_type=F32, precision=HIGHEST) + b_ref[...]


def _modulation(c, w_mod, b_mod):
    B = c.shape[0]
    n = w_mod.shape[1]
    tn = D_MODEL
    return pl.pallas_call(
        _mod_kernel,
        grid=(n // tn,),
        in_specs=[
            pl.BlockSpec((B, D_MODEL), lambda j: (0, 0)),
            pl.BlockSpec((D_MODEL, tn), lambda j: (0, j)),
            pl.BlockSpec((1, tn), lambda j: (0, j)),
        ],
        out_specs=pl.BlockSpec((B, tn), lambda j: (0, j)),
        out_shape=jax.ShapeDtypeStruct((B, n), F32),
        compiler_params=_cparams(("parallel",)),
    )(c, w_mod, b_mod.reshape(1, n))


def _inproj_kernel(x_ref, mod_ref, g_ref, w_ref, o_ref, h_scr):
    @pl.when(pl.program_id(2) == 0)
    def _():
        x = x_ref[...]
        y = x * lax.rsqrt(jnp.mean(x * x, axis=-1, keepdims=True) + EPS) * g_ref[...]
        h = y * (1.0 + mod_ref[1:2, :]) + mod_ref[0:1, :]
        h_scr[...] = h.astype(BF16)

    o_ref[...] = jnp.dot(h_scr[...], w_ref[...], preferred_element_type=F32)


def _inproj(x, mod3, g1, w_in_bf):
    B, S, _ = x.shape
    tm = min(S, 1024)
    tn = 2176
    return pl.pallas_call(
        _inproj_kernel,
        grid=(B, S // tm, IN_COLS // tn),
        in_specs=[
            pl.BlockSpec((None, tm, D_MODEL), lambda b, i, j: (b, i, 0)),
            pl.BlockSpec((None, 6, D_MODEL), lambda b, i, j: (b, 0, 0)),
            pl.BlockSpec((1, D_MODEL), lambda b, i, j: (0, 0)),
            pl.BlockSpec((D_MODEL, tn), lambda b, i, j: (0, j)),
        ],
        out_specs=pl.BlockSpec((None, tm, tn), lambda b, i, j: (b, i, j)),
        out_shape=jax.ShapeDtypeStruct((B, S, IN_COLS), F32),
        scratch_shapes=[pltpu.VMEM((tm, D_MODEL), BF16)],
        compiler_params=_cparams(("parallel", "parallel", "arbitrary")),
    )(x, mod3, g1, w_in_bf)


ATT_ROW_SPLIT = 1


def _attn_kernel(sink_ref, q_ref, kp_ref, kc_ref, kn_ref, vp_ref, vc_ref, vn_ref, o_ref, *, nblk):
    j = pl.program_id(1)
    blk = WINDOW
    span = 3 * blk
    scale = 1.0 / math.sqrt(ATT_HEAD_DIM)
    q = (q_ref[...] * scale).astype(BF16)
    k = jnp.concatenate([kp_ref[...], kc_ref[...], kn_ref[...]], axis=0).astype(BF16)
    v = jnp.concatenate([vp_ref[...], vc_ref[...], vn_ref[...]], axis=0).astype(BF16)
    rows = blk // ATT_ROW_SPLIT
    qi = lax.broadcasted_iota(jnp.int32, (blk, span), 0)
    kj = lax.broadcasted_iota(jnp.int32, (blk, span), 1)
    dist = jnp.abs(kj - blk - qi)
    lo = jnp.where(j == 0, blk, 0)
    hi = jnp.where(j == nblk - 1, 2 * blk, span)
    valid = (dist <= WINDOW) & (kj >= lo) & (kj < hi)
    distf = dist.astype(F32)
    outs = []
    for hh in range(ATT_HEADS):
        h = hh // ATT_GROUP
        slope = 2.0 ** (-8.0 * (hh + 1) / ATT_HEADS)
        kh = k[:, h * ATT_HEAD_DIM:(h + 1) * ATT_HEAD_DIM]
        vh = v[:, h * ATT_HEAD_DIM:(h + 1) * ATT_HEAD_DIM]
        sink = sink_ref[hh]
        parts = []
        for r0 in range(0, blk, rows):
            qh = q[r0:r0 + rows, hh * ATT_HEAD_DIM:(hh + 1) * ATT_HEAD_DIM]
            s = _nt_dot(qh, kh) - slope * distf[r0:r0 + rows]
            s = jnp.where(valid[r0:r0 + rows], s, -jnp.inf)
            m = jnp.maximum(jnp.max(s, axis=-1, keepdims=True), sink)
            p = jnp.exp(s - m)
            denom = jnp.sum(p, axis=-1, keepdims=True) + jnp.exp(sink - m)
            parts.append(jnp.dot(p.astype(BF16), vh, preferred_element_type=F32) / denom)
        outs.append(jnp.concatenate(parts, axis=0) if len(parts) > 1 else parts[0])
    o_ref[...] = jnp.concatenate(outs, axis=-1).astype(o_ref.dtype)


def _attention(proj, sink):
    B, S, _ = proj.shape
    blk = WINDOW
    nblk = S // blk
    kcol = COL_KA // ATT_KV
    vcol = COL_VA // ATT_KV

    def kv_spec(col, off):
        return pl.BlockSpec(
            (None, blk, ATT_KV),
            lambda b, j: (b, jnp.clip(j + off, 0, nblk - 1), col))

    return pl.pallas_call(
        functools.partial(_attn_kernel, nblk=nblk),
        grid=(B, nblk),
        in_specs=[
            pl.BlockSpec(memory_space=pltpu.SMEM),
            pl.BlockSpec((None, blk, ATT_Q), lambda b, j: (b, j, 0)),
            kv_spec(kcol, -1), kv_spec(kcol, 0), kv_spec(kcol, 1),
            kv_spec(vcol, -1), kv_spec(vcol, 0), kv_spec(vcol, 1),
        ],
        out_specs=pl.BlockSpec((None, blk, ATT_Q), lambda b, j: (b, j, 0)),
        out_shape=jax.ShapeDtypeStruct((B, S, ATT_Q), BF16),
        compiler_params=_cparams(("parallel", "parallel")),
    )(sink, proj, proj, proj, proj, proj, proj, proj)


def _hgrn_chunk(q_raw, z, v, lb, st_scr, reverse):
    C = HG_CHUNK
    q = q_raw * jax.nn.sigmoid(q_raw)
    f = lb + (1.0 - lb) * jax.nn.sigmoid(z)
    kk = (1.0 - lb) * jax.nn.sigmoid(-z)
    logf = jnp.log(f)
    row = lax.broadcasted_iota(jnp.int32, (C, C), 0)
    col = lax.broadcasted_iota(jnp.int32, (C, C), 1)
    tri = jnp.where((row <= col) if reverse else (row >= col), 1.0, 0.0).astype(F32)
    b = jnp.dot(tri, logf, preferred_element_type=F32, precision=HIGHEST)
    edge = b[0:1, :] if reverse else b[C - 1:C, :]
    vb = v.astype(BF16)

    st = st_scr[...]
    o = _nt_dot((q * jnp.exp(b)).astype(BF16), st.astype(BF16))
    kh = (kk * jnp.exp(edge - b)).astype(BF16)
    st_scr[...] = st * jnp.exp(edge) + _tn_dot(vb, kh)

    pos = lax.broadcasted_iota(jnp.int32, (C, 1), 0)
    a = jnp.zeros((C, C), F32)
    for hsz in HG_LEVELS:
        seg = 2 * hsz
        refs = []
        for s0 in range(0, C, seg):
            r = s0 + hsz if reverse else s0 + hsz - 1
            refs.append(jnp.broadcast_to(b[r:r + 1, :], (seg, HGRN_DK)))
        bmid = jnp.concatenate(refs, axis=0) if len(refs) > 1 else refs[0]
        e = jnp.exp(-jnp.abs(b - bmid))
        upper = (pos % seg) >= hsz
        qrole = jnp.logical_not(upper) if reverse else upper
        ql = jnp.where(qrole, q * e, 0.0).astype(BF16)
        kl = jnp.where(qrole, 0.0, kk * e).astype(BF16)
        al = _nt_dot(ql, kl)
        same = (row // seg) == (col // seg)
        rq = (row % seg) >= hsz
        ck = (col % seg) >= hsz
        if reverse:
            m = same & jnp.logical_not(rq) & ck
        else:
            m = same & rq & jnp.logical_not(ck)
        a = jnp.where(m, al, a)
    nb = C // HG_SUB
    shape3 = (nb, HG_SUB, HGRN_DK)
    b3 = b.reshape(shape3)
    q3 = q.reshape(shape3)
    k3 = kk.reshape(shape3)
    sub = lax.broadcasted_iota(jnp.int32, shape3, 1)
    block_col0 = (row // HG_SUB) * HG_SUB
    for s in range(HG_SUB):
        arg = jnp.minimum(b3 - b3[:, s:s + 1, :], 0.0)
        keep = (sub <= s) if reverse else (sub >= s)
        g = jnp.where(keep, q3 * k3[:, s:s + 1, :] * jnp.exp(arg), 0.0)
        w = jnp.sum(g, axis=-1, keepdims=True).reshape(C, 1)
        a = jnp.where(col == block_col0 + s, w, a)
    return o + jnp.dot(a.astype(BF16), vb, preferred_element_type=F32)


def _lower_bound(plb_ref):
    p = plb_ref[...]
    e = jnp.exp(p - jnp.max(p, axis=0, keepdims=True))
    return e[0:1, :] / jnp.sum(e, axis=0, keepdims=True)


def _hgrn_fwd_kernel(q_ref, z_ref, v_ref, plb_ref, o_ref, st_scr, *, nchunk):
    @pl.when(pl.program_id(2) == 0)
    def _():
        st_scr[...] = jnp.zeros_like(st_scr)

    lb = _lower_bound(plb_ref)
    C = HG_CHUNK
    for ci in range(nchunk):
        sl = slice(ci * C, (ci + 1) * C)
        o_ref[sl, :] = _hgrn_chunk(q_ref[sl, :], z_ref[sl, :], v_ref[sl, :], lb, st_scr, False)


def _hgrn_bwd_kernel(q_ref, z_ref, v_ref, plb_ref, of_ref, gh_ref, gn_ref, o_ref, st_scr, *, nchunk):
    @pl.when(pl.program_id(2) == 0)
    def _():
        st_scr[...] = jnp.zeros_like(st_scr)

    lb = _lower_bound(plb_ref)
    C = HG_CHUNK
    for ci in reversed(range(nchunk)):
        sl = slice(ci * C, (ci + 1) * C)
        o = _hgrn_chunk(q_ref[sl, :], z_ref[sl, :], v_ref[sl, :], lb, st_scr, True)
        o = o + of_ref[sl, :]
        o = o * lax.rsqrt(jnp.mean(o * o, axis=-1, keepdims=True) + EPS) * gn_ref[...]
        gh = gh_ref[sl, :]
        o_ref[sl, :] = (o * (gh * jax.nn.sigmoid(gh))).astype(o_ref.dtype)


def _hgrn(proj, p_lb_fwd, p_lb_bwd, g_hnorm):
    B, S, _ = proj.shape
    tb = min(S, 1024)
    nt = S // tb
    nchunk = tb // HG_CHUNK
    dk = HGRN_DK
    scratch = [pltpu.VMEM((dk, dk), F32)]

    def col_spec(col0, rev):
        if rev:
            return pl.BlockSpec((None, tb, dk), lambda b, h, t: (b, nt - 1 - t, col0 // dk + h))
        return pl.BlockSpec((None, tb, dk), lambda b, h, t: (b, t, col0 // dk + h))

    plb_spec = pl.BlockSpec((p_lb_fwd.shape[0], dk), lambda b, h, t: (0, h))
    sem = ("parallel", "parallel", "arbitrary")
    o_f = pl.pallas_call(
        functools.partial(_hgrn_fwd_kernel, nchunk=nchunk),
        grid=(B, HGRN_HEADS, nt),
        in_specs=[col_spec(COL_QH, False), col_spec(COL_ZF, False), col_spec(COL_IH, False), plb_spec],
        out_specs=pl.BlockSpec((None, tb, dk), lambda b, h, t: (b, t, h)),
        out_shape=jax.ShapeDtypeStruct((B, S, HG_V), F32),
        scratch_shapes=scratch,
        compiler_params=_cparams(sem),
    )(proj, proj, proj, p_lb_fwd)
    return pl.pallas_call(
        functools.partial(_hgrn_bwd_kernel, nchunk=nchunk),
        grid=(B, HGRN_HEADS, nt),
        in_specs=[col_spec(COL_QH, True), col_spec(COL_ZB, True), col_spec(COL_IH, True), plb_spec,
                  pl.BlockSpec((None, tb, dk), lambda b, h, t: (b, nt - 1 - t, h)),
                  col_spec(COL_GH, True),
                  pl.BlockSpec((1, dk), lambda b, h, t: (0, h))],
        out_specs=pl.BlockSpec((None, tb, dk), lambda b, h, t: (b, nt - 1 - t, h)),
        out_shape=jax.ShapeDtypeStruct((B, S, HG_V), BF16),
        scratch_shapes=scratch,
        compiler_params=_cparams(sem),
    )(proj, proj, proj, p_lb_bwd, o_f, proj, g_hnorm)


def _outproj_kernel(att_ref, hg_ref, ga0_ref, ga1_ref, gb0_ref, gb1_ref, x_ref, mod_ref, woa_ref, woh_ref,
                    wout_ref, g2_ref, wpq_ref, x1_ref, h2_ref, qp_ref):
    ya = jnp.dot(att_ref[...], woa_ref[...], preferred_element_type=F32)
    yh = jnp.dot(hg_ref[...], woh_ref[...], preferred_element_type=F32)
    ga = jnp.concatenate([ga0_ref[...], ga1_ref[...]], axis=-1)
    gb = jnp.concatenate([gb0_ref[...], gb1_ref[...]], axis=-1)
    merged = jax.nn.sigmoid(ga) * ya + jax.nn.sigmoid(gb) * yh
    x1 = x_ref[...] + mod_ref[2:3, :] * jnp.dot(merged.astype(BF16), wout_ref[...],
                                                preferred_element_type=F32)
    x1_ref[...] = x1
    y = x1 * lax.rsqrt(jnp.mean(x1 * x1, axis=-1, keepdims=True) + EPS) * g2_ref[...]
    h2 = y * (1.0 + mod_ref[4:5, :]) + mod_ref[3:4, :]
    h2_ref[...] = h2
    qp_ref[...] = jnp.dot(h2.astype(BF16), wpq_ref[...], preferred_element_type=F32)


def _outproj(att, hg, proj, x, mod3, woa, woh, wout, g2, wpq):
    B, S, _ = x.shape
    tm = 256
    half = D_MODEL // 2
    nq = wpq.shape[1]

    def tok(width, col):
        return pl.BlockSpec((None, tm, width), lambda b, i: (b, i, col))

    def full(shape):
        return pl.BlockSpec(shape, lambda b, i: (0, 0))

    return pl.pallas_call(
        _outproj_kernel,
        grid=(B, S // tm),
        in_specs=[
            tok(ATT_Q, 0), tok(HG_V, 0),
            tok(half, COL_GA // half), tok(half, COL_GA // half + 1),
            tok(half, COL_GB // half), tok(half, COL_GB // half + 1),
            tok(D_MODEL, 0),
            pl.BlockSpec((None, 6, D_MODEL), lambda b, i: (b, 0, 0)),
            full((ATT_Q, D_MODEL)), full((HG_V, D_MODEL)), full((D_MODEL, D_MODEL)),
            full((1, D_MODEL)), full((D_MODEL, nq)),
        ],
        out_specs=[tok(D_MODEL, 0), tok(D_MODEL, 0), tok(nq, 0)],
        out_shape=[jax.ShapeDtypeStruct((B, S, D_MODEL), F32),
                   jax.ShapeDtypeStruct((B, S, D_MODEL), F32),
                   jax.ShapeDtypeStruct((B, S, nq), F32)],
        compiler_params=_cparams(("parallel", "parallel")),
    )(att, hg, proj, proj, proj, proj, x, mod3, woa, woh, wout, g2, wpq)


ROUTE_HEADS = 4
ROUTE_TT = 256


def _route_kernel(q_ref, keys_ref, eid_ref, gate_ref, sv_scr, si_scr, cand_scr, cid_scr, ts_scr):
    tt = q_ref.shape[0]
    n = PEER_NKEYS
    k = PEER_TOPK
    nch = 2 * ROUTE_HEADS
    q = q_ref[...].astype(BF16)
    iota_n = lax.broadcasted_iota(jnp.int32, (n, tt), 0).astype(F32)
    s = [_nt_dot(keys_ref[c * n:(c + 1) * n, :].astype(BF16), q[:, c * PEER_HALF:(c + 1) * PEER_HALF])
         for c in range(nch)]
    for i in range(k):
        for c in range(nch):
            m = jnp.max(s[c], axis=0, keepdims=True)
            tagged = jnp.where(s[c] == m, iota_n, float(n))
            idx = jnp.min(tagged, axis=0, keepdims=True)
            sv_scr[c, i:i + 1, :] = m
            si_scr[c, i:i + 1, :] = idx
            s[c] = jnp.where(tagged == idx, -jnp.inf, s[c])
    ncand = cand_scr.shape[1]
    cand_scr[...] = jnp.full(cand_scr.shape, -jnp.inf, F32)
    cid_scr[...] = jnp.zeros(cid_scr.shape, F32)
    for h in range(ROUTE_HEADS):
        off = 0
        for a in range(k):
            nb = k // (a + 1)
            cand_scr[h, off:off + nb, :] = sv_scr[2 * h, a:a + 1, :] + sv_scr[2 * h + 1, 0:nb, :]
            cid_scr[h, off:off + nb, :] = si_scr[2 * h, a:a + 1, :] * float(n) + si_scr[2 * h + 1, 0:nb, :]
            off += nb
    cs = [cand_scr[h] for h in range(ROUTE_HEADS)]
    ids = [cid_scr[h] for h in range(ROUTE_HEADS)]
    iota_c = lax.broadcasted_iota(jnp.int32, (ncand, tt), 0).astype(F32)
    for i in range(k):
        for h in range(ROUTE_HEADS):
            m = jnp.max(cs[h], axis=0, keepdims=True)
            tagged = jnp.where(cs[h] == m, iota_c, float(ncand))
            sel = tagged == jnp.min(tagged, axis=0, keepdims=True)
            r = h * k + i
            eid_ref[r:r + 1, :] = jnp.max(jnp.where(sel, ids[h], -1.0), axis=0, keepdims=True).astype(jnp.int32)
            ts_scr[r:r + 1, :] = m
            cs[h] = jnp.where(sel, -jnp.inf, cs[h])
    for h in range(ROUTE_HEADS):
        ts = ts_scr[h * k:(h + 1) * k, :]
        ex = jnp.exp(ts - ts[0:1, :])
        gate_ref[h * k:(h + 1) * k, :] = ex / jnp.sum(ex, axis=0, keepdims=True)


def _route(qp, keys2d):
    n_tok = qp.shape[0]
    tt = ROUTE_TT
    hp = ROUTE_HEADS
    k = PEER_TOPK
    npairs = sum(k // (a + 1) for a in range(k))
    ncand = -(-npairs // SUBLANES) * SUBLANES
    return pl.pallas_call(
        _route_kernel,
        grid=(n_tok // tt, PEER_HEADS // hp),
        in_specs=[
            pl.BlockSpec((tt, hp * 2 * PEER_HALF), lambda i, h: (i, h)),
            pl.BlockSpec((hp * 2 * PEER_NKEYS, PEER_HALF), lambda i, h: (h, 0)),
        ],
        out_specs=[pl.BlockSpec((hp * k, tt), lambda i, h: (h, i)),
                   pl.BlockSpec((hp * k, tt), lambda i, h: (h, i))],
        out_shape=[jax.ShapeDtypeStruct((PEER_SEL, n_tok), jnp.int32),
                   jax.ShapeDtypeStruct((PEER_SEL, n_tok), F32)],
        scratch_shapes=[pltpu.VMEM((2 * hp, k, tt), F32), pltpu.VMEM((2 * hp, k, tt), F32),
                        pltpu.VMEM((hp, ncand, tt), F32), pltpu.VMEM((hp, ncand, tt), F32),
                        pltpu.VMEM((hp * k, tt), F32)],
        compiler_params=_cparams(("parallel", "arbitrary")),
    )(qp, keys2d)


PEER_GROUP = SUBLANES
PEER_NGROUP = 4
PEER_TT = PEER_GROUP * PEER_NGROUP
U_MASK = 0xFFFF0000


PEER_CHUNKS = D_MODEL // LANES
PEER_SEL_TILES = PEER_SEL // SUBLANES


def _pack_tables(u, v):
    ub = lax.bitcast_convert_type(u.astype(BF16), jnp.uint16).astype(jnp.uint32)
    vb = lax.bitcast_convert_type(v.astype(BF16), jnp.uint16).astype(jnp.uint32)
    return ((ub << 16) | vb).reshape(u.shape[0], PEER_CHUNKS, LANES)


def _peer_kernel(eid_ref, eidn_ref, gate_ref, h2_ref, x1_ref, mod_ref, gf_ref, tab_hbm, o_ref, wbuf, sem, w_scr):
    i = pl.program_id(0)
    n = pl.num_programs(0)
    slot = i % 2
    G = PEER_GROUP

    def issue_rows(idx_ref, g, s, tok, j0, j1):
        for j in range(j0, j1):
            e = idx_ref[tok, j]
            pltpu.make_async_copy(tab_hbm.at[e], wbuf.at[s, tok, j], sem.at[s, g]).start(priority=j % 2)

    def wait_group(g, s):
        rows = pl.ds(g * G, G)
        pltpu.make_async_copy(wbuf.at[s, rows], wbuf.at[s, rows], sem.at[s, g]).wait()

    per_step = PEER_SEL // (2 * PEER_SEL_TILES)
    sub8 = lax.broadcasted_iota(jnp.int32, (SUBLANES, LANES), 0)

    def fold(x, y, h):
        low = (sub8 % (2 * h)) < h
        if 2 * h == SUBLANES:
            return jnp.where(low, x, y) + pltpu.roll(jnp.where(low, y, x), h, 0)
        return jnp.where(low, x + pltpu.roll(x, SUBLANES - h, 0), y + pltpu.roll(y, h, 0))

    fold_order = (0, 4, 2, 6, 1, 5, 3, 7)

    def chunk_sums(tiles):
        t = [tiles[k] for k in fold_order]
        t = [fold(t[0], t[1], 4), fold(t[2], t[3], 4), fold(t[4], t[5], 4), fold(t[6], t[7], 4)]
        t = [fold(t[0], t[1], 2), fold(t[2], t[3], 2)]
        return fold(t[0], t[1], 1)

    def compute_group(g, s):
        rows = pl.ds(pl.multiple_of(g * G, G), G)
        gate_t = gate_ref[rows, :].T
        h2g = h2_ref[rows, :]
        sub = lax.broadcasted_iota(jnp.int32, (G, D_MODEL), 0)
        y = jnp.zeros((G, D_MODEL), F32)
        for r in range(G):
            tok = g * G + r
            h2t = jnp.zeros((SUBLANES, LANES), F32)
            for c in range(PEER_CHUNKS):
                h2t = jnp.where(sub8 == c, h2g[r:r + 1, c * LANES:(c + 1) * LANES], h2t)
            part = []
            for a in range(PEER_SEL_TILES):
                issue_rows(eidn_ref, g, 1 - s, tok, a * per_step, (a + 1) * per_step)
                part.append(chunk_sums([
                    lax.bitcast_convert_type(wbuf[s, tok, a * SUBLANES + b] & jnp.uint32(U_MASK), F32) * h2t
                    for b in range(SUBLANES)]))
            act = jnp.sum(jnp.stack(part), axis=-1, keepdims=True)
            act = 0.5 * act * (1.0 + lax.erf(act * (1.0 / math.sqrt(2.0))))
            w = gate_t[:, r:r + 1].reshape(PEER_SEL_TILES, SUBLANES, 1) * act
            w_scr[...] = jnp.broadcast_to(w, (PEER_SEL_TILES, SUBLANES, LANES))
            acc = jnp.zeros((SUBLANES, LANES), F32)
            for a in range(PEER_SEL_TILES):
                issue_rows(eidn_ref, g, 1 - s, tok, (PEER_SEL_TILES + a) * per_step,
                           (PEER_SEL_TILES + a + 1) * per_step)
                for b in range(SUBLANES):
                    vf = lax.bitcast_convert_type(wbuf[s, tok, a * SUBLANES + b] << 16, F32)
                    acc = acc + w_scr[a, b:b + 1, :] * vf
            yrow = jnp.concatenate([acc[c:c + 1, :] for c in range(PEER_CHUNKS)], axis=-1)
            y = jnp.where(sub == r, yrow, y)
        x2 = x1_ref[rows, :] + mod_ref[5:6, :] * y
        o_ref[rows, :] = x2 * lax.rsqrt(jnp.mean(x2 * x2, axis=-1, keepdims=True) + EPS) * gf_ref[...]

    @pl.when(i == 0)
    def _():
        def first(t, carry):
            issue_rows(eid_ref, t // G, 0, t, 0, PEER_SEL)
            return carry
        lax.fori_loop(0, PEER_TT, first, 0)

    def step(g, carry):
        wait_group(g, slot)
        compute_group(g, slot)
        return carry

    lax.fori_loop(0, PEER_NGROUP, step, 0)

    @pl.when(i == n - 1)
    def _():
        def drain(g, carry):
            wait_group(g, 1 - slot)
            return carry
        lax.fori_loop(0, PEER_NGROUP, drain, 0)


def _peer(eid_t, gate_t, h2, x1, mod3, g_final, table, seq):
    n_tok = h2.shape[0]
    tt = PEER_TT
    n = n_tok // tt
    smem_idx = lambda f: pl.BlockSpec((tt, PEER_SEL), f, memory_space=pltpu.SMEM)
    tok = lambda w: pl.BlockSpec((tt, w), lambda i: (i, 0))
    return pl.pallas_call(
        _peer_kernel,
        grid=(n,),
        in_specs=[
            smem_idx(lambda i: (i, 0)),
            smem_idx(lambda i: (jnp.minimum(i + 1, n - 1), 0)),
            tok(PEER_SEL), tok(D_MODEL), tok(D_MODEL),
            pl.BlockSpec((None, 6, D_MODEL), lambda i: ((i * tt) // seq, 0, 0)),
            pl.BlockSpec((1, D_MODEL), lambda i: (0, 0)),
            pl.BlockSpec(memory_space=pl.ANY),
        ],
        out_specs=tok(D_MODEL),
        out_shape=jax.ShapeDtypeStruct((n_tok, D_MODEL), F32),
        scratch_shapes=[pltpu.VMEM((2, tt, PEER_SEL, PEER_CHUNKS, LANES), jnp.uint32),
                        pltpu.SemaphoreType.DMA((2, PEER_NGROUP)),
                        pltpu.VMEM((PEER_SEL_TILES, SUBLANES, LANES), F32)],
        compiler_params=_cparams(("arbitrary",)),
    )(eid_t, eid_t, gate_t, h2, x1, mod3, g_final, table)


def _run_group(x, c, w_mod, b_mod, g1, w_in_bf, sink, p_lb_fwd, p_lb_bwd, g_hnorm, woa, woh, wout, g2,
               wpq, keys2d, table, g_final):
    B, S, _ = x.shape
    mod3 = _modulation(c, w_mod, b_mod).reshape(B, 6, D_MODEL)
    proj = _inproj(x, mod3, g1, w_in_bf)
    att = _attention(proj, sink)
    hg = _hgrn(proj, p_lb_fwd, p_lb_bwd, g_hnorm)
    x1, h2, qp = _outproj(att, hg, proj, x, mod3, woa, woh, wout, g2, wpq)
    n_tok = B * S
    eid, gate = _route(qp.reshape(n_tok, -1), keys2d)
    y = _peer(eid.T, gate.T, h2.reshape(n_tok, D_MODEL), x1.reshape(n_tok, D_MODEL), mod3, g_final, table, S)
    return y.reshape(B, S, D_MODEL)


def kernel(x_prompt, x_sample, c_prompt, c_sample, w_mod, b_mod, g_norm1, w_in, att_sink, p_lb_fwd, p_lb_bwd,
           g_hnorm, w_o_att, w_o_hgrn, w_out, g_norm2, w_pq, peer_keys, peer_u, peer_v, g_final):
    assert w_mod.shape[0] == 1 and p_lb_fwd.shape[0] == 2, "single-layer encoder"
    shared = (
        w_mod[0], b_mod[0], g_norm1[0].reshape(1, D_MODEL), w_in[0].astype(BF16), att_sink[0],
        p_lb_fwd, p_lb_bwd, g_hnorm[0].reshape(1, HG_V),
        w_o_att[0].astype(BF16), w_o_hgrn[0].astype(BF16), w_out[0].astype(BF16),
        g_norm2[0].reshape(1, D_MODEL), w_pq[0].astype(BF16),
        peer_keys[0].reshape(PEER_HEADS * 2 * PEER_NKEYS, PEER_HALF),
        _pack_tables(peer_u[0], peer_v[0]), g_final.reshape(1, D_MODEL),
    )
    return (_run_group(x_prompt, c_prompt, *shared), _run_group(x_sample, c_sample, *shared))
```

```python
import functools
import math

import jax
import jax.numpy as jnp
from jax import lax
from jax.experimental import pallas as pl
from jax.experimental.pallas import tpu as pltpu

F32 = jnp.float32
BF16 = jnp.bfloat16
HIGHEST = lax.Precision.HIGHEST

D_MODEL = 1024
ATT_HEADS = 16
ATT_KV_HEADS = 4
ATT_HEAD_DIM = 64
ATT_GROUP = ATT_HEADS // ATT_KV_HEADS
WINDOW = 128
ATT_Q = ATT_HEADS * ATT_HEAD_DIM
ATT_KV = ATT_KV_HEADS * ATT_HEAD_DIM
HGRN_HEADS = 8
HGRN_DK = 128
HG_K = HGRN_HEADS * HGRN_DK
HG_V = HG_K
IN_COLS = ATT_Q + 2 * ATT_KV + 3 * HG_K + 2 * HG_V + 2 * D_MODEL
PEER_HEADS = 8
PEER_NKEYS = 128
PEER_HALF = 128
PEER_TOPK = 16
PEER_SEL = PEER_HEADS * PEER_TOPK
EPS = 1e-6

COL_QA = 0
COL_KA = COL_QA + ATT_Q
COL_VA = COL_KA + ATT_KV
COL_QH = COL_VA + ATT_KV
COL_ZF = COL_QH + HG_K
COL_ZB = COL_ZF + HG_K
COL_IH = COL_ZB + HG_K
COL_GH = COL_IH + HG_V
COL_GA = COL_GH + HG_V
COL_GB = COL_GA + D_MODEL

LANES = 128
SUBLANES = 8
VMEM_LIMIT = 56 * 1024 * 1024

HG_CHUNK = 128
HG_SUB = SUBLANES
HG_LEVELS = (64, 32, 16, 8)
HG_HEADS_PER_STEP = 1


def _cparams(sem):
    return pltpu.CompilerParams(dimension_semantics=sem, vmem_limit_bytes=VMEM_LIMIT)


def _nt_dot(a, b):
    return lax.dot_general(a, b, (((1,), (1,)), ((), ())), preferred_element_type=F32)


def _tn_dot(a, b):
    return lax.dot_general(a, b, (((0,), (0,)), ((), ())), preferred_element_type=F32)


def _mod_kernel(c_ref, w_ref, b_ref, o_ref):
    c = c_ref[...]
    sc = c * jax.nn.sigmoid(c)
    o_ref[...] = jnp.dot(sc, w_ref[...], preferred_element_type=F32, precision=HIGHEST) + b_ref[...]


def _modulation(c, w_mod, b_mod):
    B = c.shape[0]
    n = w_mod.shape[1]
    tn = D_MODEL
    return pl.pallas_call(
        _mod_kernel,
        grid=(n // tn,),
        in_specs=[
            pl.BlockSpec((B, D_MODEL), lambda j: (0, 0)),
            pl.BlockSpec((D_MODEL, tn), lambda j: (0, j)),
            pl.BlockSpec((1, tn), lambda j: (0, j)),
        ],
        out_specs=pl.BlockSpec((B, tn), lambda j: (0, j)),
        out_shape=jax.ShapeDtypeStruct((B, n), F32),
        compiler_params=_cparams(("parallel",)),
    )(c, w_mod, b_mod.reshape(1, n))


def _inproj_kernel(x_ref, mod_ref, g_ref, w_ref, o_ref, h_scr):
    @pl.when(pl.program_id(2) == 0)
    def _():
        x = x_ref[...]
        y = x * lax.rsqrt(jnp.mean(x * x, axis=-1, keepdims=True) + EPS) * g_ref[...]
        h = y * (1.0 + mod_ref[1:2, :]) + mod_ref[0:1, :]
        h_scr[...] = h.astype(BF16)

    o_ref[...] = jnp.dot(h_scr[...], w_ref[...], preferred_element_type=F32)


def _inproj(x, mod3, g1, w_in_bf):
    B, S, _ = x.shape
    tm = min(S, 1024)
    tn = 2176
    return pl.pallas_call(
        _inproj_kernel,
        grid=(B, S // tm, IN_COLS // tn),
        in_specs=[
            pl.BlockSpec((None, tm, D_MODEL), lambda b, i, j: (b, i, 0)),
            pl.BlockSpec((None, 6, D_MODEL), lambda b, i, j: (b, 0, 0)),
            pl.BlockSpec((1, D_MODEL), lambda b, i, j: (0, 0)),
            pl.BlockSpec((D_MODEL, tn), lambda b, i, j: (0, j)),
        ],
        out_specs=pl.BlockSpec((None, tm, tn), lambda b, i, j: (b, i, j)),
        out_shape=jax.ShapeDtypeStruct((B, S, IN_COLS), F32),
        scratch_shapes=[pltpu.VMEM((tm, D_MODEL), BF16)],
        compiler_params=_cparams(("parallel", "parallel", "arbitrary")),
    )(x, mod3, g1, w_in_bf)


def _attn_kernel(sink_ref, q_ref, kp_ref, kc_ref, kn_ref, vp_ref, vc_ref, vn_ref, o_ref, s_scr, p_scr, r_scr,
                 *, nblk):
    j = pl.program_id(1)
    blk = WINDOW
    span = 3 * blk
    scale = 1.0 / math.sqrt(ATT_HEAD_DIM)
    q = (q_ref[...] * scale).astype(BF16)
    k = jnp.concatenate([kp_ref[...], kc_ref[...], kn_ref[...]], axis=0).astype(BF16)
    v = jnp.concatenate([vp_ref[...], vc_ref[...], vn_ref[...]], axis=0).astype(BF16)
    qi = lax.broadcasted_iota(jnp.int32, (blk, span), 0)
    kj = lax.broadcasted_iota(jnp.int32, (blk, span), 1)
    dist = jnp.abs(kj - blk - qi)
    lo = jnp.where(j == 0, blk, 0)
    hi = jnp.where(j == nblk - 1, 2 * blk, span)
    valid = (dist <= WINDOW) & (kj >= lo) & (kj < hi)
    distf = dist.astype(F32)
    for hh in range(ATT_HEADS):
        h = hh // ATT_GROUP
        slope = 2.0 ** (-8.0 * (hh + 1) / ATT_HEADS)
        qh = q[:, hh * ATT_HEAD_DIM:(hh + 1) * ATT_HEAD_DIM]
        kh = k[:, h * ATT_HEAD_DIM:(h + 1) * ATT_HEAD_DIM]
        s = _nt_dot(qh, kh) - slope * distf
        s_scr[hh] = jnp.where(valid, s, -jnp.inf)
    for hh in range(ATT_HEADS):
        s = s_scr[hh]
        sink = sink_ref[hh]
        m = jnp.maximum(jnp.max(s, axis=-1, keepdims=True), sink)
        p = jnp.exp(s - m)
        r_scr[hh] = 1.0 / (jnp.sum(p, axis=-1, keepdims=True) + jnp.exp(sink - m))
        p_scr[hh] = p.astype(BF16)
    outs = []
    for hh in range(ATT_HEADS):
        h = hh // ATT_GROUP
        vh = v[:, h * ATT_HEAD_DIM:(h + 1) * ATT_HEAD_DIM]
        outs.append(jnp.dot(p_scr[hh], vh, preferred_element_type=F32) * r_scr[hh])
    o_ref[...] = jnp.concatenate(outs, axis=-1).astype(o_ref.dtype)


def _attention(proj, sink):
    B, S, _ = proj.shape
    blk = WINDOW
    nblk = S // blk
    kcol = COL_KA // ATT_KV
    vcol = COL_VA // ATT_KV

    def kv_spec(col, off):
        return pl.BlockSpec(
            (None, blk, ATT_KV),
            lambda b, j: (b, jnp.clip(j + off, 0, nblk - 1), col))

    return pl.pallas_call(
        functools.partial(_attn_kernel, nblk=nblk),
        grid=(B, nblk),
        in_specs=[
            pl.BlockSpec(memory_space=pltpu.SMEM),
            pl.BlockSpec((None, blk, ATT_Q), lambda b, j: (b, j, 0)),
            kv_spec(kcol, -1), kv_spec(kcol, 0), kv_spec(kcol, 1),
            kv_spec(vcol, -1), kv_spec(vcol, 0), kv_spec(vcol, 1),
        ],
        out_specs=pl.BlockSpec((None, blk, ATT_Q), lambda b, j: (b, j, 0)),
        out_shape=jax.ShapeDtypeStruct((B, S, ATT_Q), BF16),
        scratch_shapes=[pltpu.VMEM((ATT_HEADS, blk, 3 * blk), F32),
                        pltpu.VMEM((ATT_HEADS, blk, 3 * blk), BF16),
                        pltpu.VMEM((ATT_HEADS, blk, 1), F32)],
        compiler_params=_cparams(("parallel", "parallel")),
    )(sink, proj, proj, proj, proj, proj, proj, proj)


def _hgrn_chunk(q_raw, z, v, lb, st_scr, reverse):
    C = HG_CHUNK
    q = q_raw * jax.nn.sigmoid(q_raw)
    f = lb + (1.0 - lb) * jax.nn.sigmoid(z)
    kk = (1.0 - lb) * jax.nn.sigmoid(-z)
    logf = jnp.log(f) * (1.0 / math.log(2.0))
    row = lax.broadcasted_iota(jnp.int32, (C, C), 0)
    col = lax.broadcasted_iota(jnp.int32, (C, C), 1)
    tri = jnp.where((row <= col) if reverse else (row >= col), 1.0, 0.0).astype(F32)
    b = jnp.dot(tri, logf, preferred_element_type=F32, precision=HIGHEST)
    edge = b[0:1, :] if reverse else b[C - 1:C, :]
    vb = v.astype(BF16)

    st = st_scr[...]
    o = _nt_dot((q * jnp.exp2(b)).astype(BF16), st.astype(BF16))
    kh = (kk * jnp.exp2(edge - b)).astype(BF16)
    st_scr[...] = st * jnp.exp2(edge) + _tn_dot(vb, kh)

    pos = lax.broadcasted_iota(jnp.int32, (C, 1), 0)
    a = jnp.zeros((C, C), F32)
    for hsz in HG_LEVELS:
        seg = 2 * hsz
        refs = []
        for s0 in range(0, C, seg):
            r = s0 + hsz if reverse else s0 + hsz - 1
            refs.append(jnp.broadcast_to(b[r:r + 1, :], (seg, HGRN_DK)))
        bmid = jnp.concatenate(refs, axis=0) if len(refs) > 1 else refs[0]
        e = jnp.exp2(-jnp.abs(b - bmid))
        upper = (pos % seg) >= hsz
        qrole = jnp.logical_not(upper) if reverse else upper
        ql = jnp.where(qrole, q * e, 0.0).astype(BF16)
        kl = jnp.where(qrole, 0.0, kk * e).astype(BF16)
        al = _nt_dot(ql, kl)
        same = (row // seg) == (col // seg)
        rq = (row % seg) >= hsz
        ck = (col % seg) >= hsz
        if reverse:
            m = same & jnp.logical_not(rq) & ck
        else:
            m = same & rq & jnp.logical_not(ck)
        a = jnp.where(m, al, a)
    nb = C // HG_SUB
    shape3 = (nb, HG_SUB, HGRN_DK)
    b3 = b.reshape(shape3)
    q3 = q.reshape(shape3)
    k3 = kk.reshape(shape3)
    sub = lax.broadcasted_iota(jnp.int32, shape3, 1)
    block_col0 = (row // HG_SUB) * HG_SUB
    for s in range(HG_SUB):
        keep = (sub <= s) if reverse else (sub >= s)
        arg = jnp.where(keep, b3 - b3[:, s:s + 1, :], -jnp.inf)
        g = q3 * k3[:, s:s + 1, :] * jnp.exp2(arg)
        w = jnp.sum(g, axis=-1, keepdims=True).reshape(C, 1)
        a = jnp.where(col == block_col0 + s, w, a)
    return o + jnp.dot(a.astype(BF16), vb, preferred_element_type=F32)


def _lower_bound(plb_ref):
    p = plb_ref[...]
    e = jnp.exp(p - jnp.max(p, axis=0, keepdims=True))
    return e[0:1, :] / jnp.sum(e, axis=0, keepdims=True)


def _hgrn_fwd_kernel(q_ref, z_ref, v_ref, plb_ref, o_ref, st_scr, *, nchunk):
    @pl.when(pl.program_id(2) == 0)
    def _():
        st_scr[...] = jnp.zeros_like(st_scr)

    lb = _lower_bound(plb_ref)
    C = HG_CHUNK
    for ci in range(nchunk):
        sl = slice(ci * C, (ci + 1) * C)
        for hh in range(HG_HEADS_PER_STEP):
            cs = slice(hh * HGRN_DK, (hh + 1) * HGRN_DK)
            o_ref[sl, cs] = _hgrn_chunk(q_ref[sl, cs], z_ref[sl, cs], v_ref[sl, cs], lb[:, cs],
                                        st_scr.at[hh], False)


def _hgrn_bwd_kernel(q_ref, z_ref, v_ref, plb_ref, of_ref, gh_ref, gn_ref, o_ref, st_scr, *, nchunk):
    @pl.when(pl.program_id(2) == 0)
    def _():
        st_scr[...] = jnp.zeros_like(st_scr)

    lb = _lower_bound(plb_ref)
    C = HG_CHUNK
    for ci in reversed(range(nchunk)):
        sl = slice(ci * C, (ci + 1) * C)
        for hh in range(HG_HEADS_PER_STEP):
            cs = slice(hh * HGRN_DK, (hh + 1) * HGRN_DK)
            o = _hgrn_chunk(q_ref[sl, cs], z_ref[sl, cs], v_ref[sl, cs], lb[:, cs], st_scr.at[hh], True)
            o = o + of_ref[sl, cs]
            o = o * lax.rsqrt(jnp.mean(o * o, axis=-1, keepdims=True) + EPS) * gn_ref[:, cs]
            gh = gh_ref[sl, cs]
            o_ref[sl, cs] = (o * (gh * jax.nn.sigmoid(gh))).astype(o_ref.dtype)


def _hgrn(proj, p_lb_fwd, p_lb_bwd, g_hnorm):
    B, S, _ = proj.shape
    tb = min(S, 1024)
    nt = S // tb
    nchunk = tb // HG_CHUNK
    hp = HG_HEADS_PER_STEP
    dk = HGRN_DK
    wd = hp * dk
    scratch = [pltpu.VMEM((hp, dk, dk), F32)]

    def col_spec(col0, rev):
        if rev:
            return pl.BlockSpec((None, tb, wd), lambda b, h, t: (b, nt - 1 - t, col0 // wd + h))
        return pl.BlockSpec((None, tb, wd), lambda b, h, t: (b, t, col0 // wd + h))

    plb_spec = pl.BlockSpec((p_lb_fwd.shape[0], wd), lambda b, h, t: (0, h))
    sem = ("parallel", "parallel", "arbitrary")
    o_f = pl.pallas_call(
        functools.partial(_hgrn_fwd_kernel, nchunk=nchunk),
        grid=(B, HGRN_HEADS // hp, nt),
        in_specs=[col_spec(COL_QH, False), col_spec(COL_ZF, False), col_spec(COL_IH, False), plb_spec],
        out_specs=pl.BlockSpec((None, tb, wd), lambda b, h, t: (b, t, h)),
        out_shape=jax.ShapeDtypeStruct((B, S, HG_V), F32),
        scratch_shapes=scratch,
        compiler_params=_cparams(sem),
    )(proj, proj, proj, p_lb_fwd)
    return pl.pallas_call(
        functools.partial(_hgrn_bwd_kernel, nchunk=nchunk),
        grid=(B, HGRN_HEADS // hp, nt),
        in_specs=[col_spec(COL_QH, True), col_spec(COL_ZB, True), col_spec(COL_IH, True), plb_spec,
                  pl.BlockSpec((None, tb, wd), lambda b, h, t: (b, nt - 1 - t, h)),
                  col_spec(COL_GH, True),
                  pl.BlockSpec((1, wd), lambda b, h, t: (0, h))],
        out_specs=pl.BlockSpec((None, tb, wd), lambda b, h, t: (b, nt - 1 - t, h)),
        out_shape=jax.ShapeDtypeStruct((B, S, HG_V), BF16),
        scratch_shapes=scratch,
        compiler_params=_cparams(sem),
    )(proj, proj, proj, p_lb_bwd, o_f, proj, g_hnorm)


def _outproj_kernel(att_ref, hg_ref, ga0_ref, ga1_ref, gb0_ref, gb1_ref, x_ref, mod_ref, woa_ref, woh_ref,
                    wout_ref, g2_ref, wpq_ref, x1_ref, h2_ref, qp_ref):
    ya = jnp.dot(att_ref[...], woa_ref[...], preferred_element_type=F32)
    yh = jnp.dot(hg_ref[...], woh_ref[...], preferred_element_type=F32)
    ga = jnp.concatenate([ga0_ref[...], ga1_ref[...]], axis=-1)
    gb = jnp.concatenate([gb0_ref[...], gb1_ref[...]], axis=-1)
    merged = jax.nn.sigmoid(ga) * ya + jax.nn.sigmoid(gb) * yh
    x1 = x_ref[...] + mod_ref[2:3, :] * jnp.dot(merged.astype(BF16), wout_ref[...],
                                                preferred_element_type=F32)
    x1_ref[...] = x1
    y = x1 * lax.rsqrt(jnp.mean(x1 * x1, axis=-1, keepdims=True) + EPS) * g2_ref[...]
    h2 = y * (1.0 + mod_ref[4:5, :]) + mod_ref[3:4, :]
    h2_ref[...] = h2
    qp_ref[...] = jnp.dot(h2.astype(BF16), wpq_ref[...], preferred_element_type=F32)


def _outproj(att, hg, proj, x, mod3, woa, woh, wout, g2, wpq):
    B, S, _ = x.shape
    tm = 256
    half = D_MODEL // 2
    nq = wpq.shape[1]

    def tok(width, col):
        return pl.BlockSpec((None, tm, width), lambda b, i: (b, i, col))

    def full(shape):
        return pl.BlockSpec(shape, lambda b, i: (0, 0))

    return pl.pallas_call(
        _outproj_kernel,
        grid=(B, S // tm),
        in_specs=[
            tok(ATT_Q, 0), tok(HG_V, 0),
            tok(half, COL_GA // half), tok(half, COL_GA // half + 1),
            tok(half, COL_GB // half), tok(half, COL_GB // half + 1),
            tok(D_MODEL, 0),
            pl.BlockSpec((None, 6, D_MODEL), lambda b, i: (b, 0, 0)),
            full((ATT_Q, D_MODEL)), full((HG_V, D_MODEL)), full((D_MODEL, D_MODEL)),
            full((1, D_MODEL)), full((D_MODEL, nq)),
        ],
        out_specs=[tok(D_MODEL, 0), tok(D_MODEL, 0), tok(nq, 0)],
        out_shape=[jax.ShapeDtypeStruct((B, S, D_MODEL), F32),
                   jax.ShapeDtypeStruct((B, S, D_MODEL), F32),
                   jax.ShapeDtypeStruct((B, S, nq), F32)],
        compiler_params=_cparams(("parallel", "parallel")),
    )(att, hg, proj, proj, proj, proj, x, mod3, woa, woh, wout, g2, wpq)


ROUTE_HEADS = 4
ROUTE_TT = 256


def _route_kernel(q_ref, keys_ref, eid_ref, gate_ref, sv_scr, si_scr, cand_scr, cid_scr, ts_scr):
    tt = q_ref.shape[0]
    n = PEER_NKEYS
    k = PEER_TOPK
    nch = 2 * ROUTE_HEADS
    q = q_ref[...].astype(BF16)
    iota_n = lax.broadcasted_iota(jnp.int32, (n, tt), 0).astype(F32)
    s = [_nt_dot(keys_ref[c * n:(c + 1) * n, :].astype(BF16), q[:, c * PEER_HALF:(c + 1) * PEER_HALF])
         for c in range(nch)]
    for i in range(k):
        for c in range(nch):
            m = jnp.max(s[c], axis=0, keepdims=True)
            tagged = jnp.where(s[c] == m, iota_n, float(n))
            idx = jnp.min(tagged, axis=0, keepdims=True)
            sv_scr[c, i:i + 1, :] = m
            si_scr[c, i:i + 1, :] = idx
            s[c] = jnp.where(tagged == idx, -jnp.inf, s[c])
    ncand = cand_scr.shape[1]
    cand_scr[...] = jnp.full(cand_scr.shape, -jnp.inf, F32)
    cid_scr[...] = jnp.zeros(cid_scr.shape, F32)
    for h in range(ROUTE_HEADS):
        off = 0
        for a in range(k):
            nb = k // (a + 1)
            cand_scr[h, off:off + nb, :] = sv_scr[2 * h, a:a + 1, :] + sv_scr[2 * h + 1, 0:nb, :]
            cid_scr[h, off:off + nb, :] = si_scr[2 * h, a:a + 1, :] * float(n) + si_scr[2 * h + 1, 0:nb, :]
            off += nb
    cs = [cand_scr[h] for h in range(ROUTE_HEADS)]
    ids = [cid_scr[h] for h in range(ROUTE_HEADS)]
    iota_c = lax.broadcasted_iota(jnp.int32, (ncand, tt), 0).astype(F32)
    for i in range(k):
        for h in range(ROUTE_HEADS):
            m = jnp.max(cs[h], axis=0, keepdims=True)
            tagged = jnp.where(cs[h] == m, iota_c, float(ncand))
            sel = tagged == jnp.min(tagged, axis=0, keepdims=True)
            r = h * k + i
            eid_ref[r:r + 1, :] = jnp.max(jnp.where(sel, ids[h], -1.0), axis=0, keepdims=True).astype(jnp.int32)
            ts_scr[r:r + 1, :] = m
            cs[h] = jnp.where(sel, -jnp.inf, cs[h])
    for h in range(ROUTE_HEADS):
        ts = ts_scr[h * k:(h + 1) * k, :]
        ex = jnp.exp(ts - ts[0:1, :])
        gate_ref[h * k:(h + 1) * k, :] = ex / jnp.sum(ex, axis=0, keepdims=True)


def _route(qp, keys2d):
    n_tok = qp.shape[0]
    tt = ROUTE_TT
    hp = ROUTE_HEADS
    k = PEER_TOPK
    npairs = sum(k // (a + 1) for a in range(k))
    ncand = -(-npairs // SUBLANES) * SUBLANES
    return pl.pallas_call(
        _route_kernel,
        grid=(n_tok // tt, PEER_HEADS // hp),
        in_specs=[
            pl.BlockSpec((tt, hp * 2 * PEER_HALF), lambda i, h: (i, h)),
            pl.BlockSpec((hp * 2 * PEER_NKEYS, PEER_HALF), lambda i, h: (h, 0)),
        ],
        out_specs=[pl.BlockSpec((hp * k, tt), lambda i, h: (h, i)),
                   pl.BlockSpec((hp * k, tt), lambda i, h: (h, i))],
        out_shape=[jax.ShapeDtypeStruct((PEER_SEL, n_tok), jnp.int32),
                   jax.ShapeDtypeStruct((PEER_SEL, n_tok), F32)],
        scratch_shapes=[pltpu.VMEM((2 * hp, k, tt), F32), pltpu.VMEM((2 * hp, k, tt), F32),
                        pltpu.VMEM((hp, ncand, tt), F32), pltpu.VMEM((hp, ncand, tt), F32),
                        pltpu.VMEM((hp * k, tt), F32)],
        compiler_params=_cparams(("parallel", "arbitrary")),
    )(qp, keys2d)


PEER_GROUP = SUBLANES
PEER_NGROUP = 4
PEER_TT = PEER_GROUP * PEER_NGROUP
U_MASK = 0xFFFF0000


PEER_CHUNKS = D_MODEL // LANES
PEER_SEL_TILES = PEER_SEL // SUBLANES


def _pack_tables(u, v):
    ub = lax.bitcast_convert_type(u.astype(BF16), jnp.uint16).astype(jnp.uint32)
    vb = lax.bitcast_convert_type(v.astype(BF16), jnp.uint16).astype(jnp.uint32)
    return ((ub << 16) | vb).reshape(u.shape[0], PEER_CHUNKS, LANES)


def _peer_kernel(eid_ref, eidn_ref, gate_ref, h2_ref, x1_ref, mod_ref, gf_ref, tab_hbm, o_ref, wbuf, sem, w_scr):
    i = pl.program_id(0)
    n = pl.num_programs(0)
    slot = i % 2
    G = PEER_GROUP

    def issue_rows(idx_ref, g, s, tok, j0, j1):
        for j in range(j0, j1):
            e = idx_ref[tok, j]
            pltpu.make_async_copy(tab_hbm.at[e], wbuf.at[s, tok, j], sem.at[s, g]).start(priority=j % 2)

    def wait_group(g, s):
        rows = pl.ds(g * G, G)
        pltpu.make_async_copy(wbuf.at[s, rows], wbuf.at[s, rows], sem.at[s, g]).wait()

    per_step = PEER_SEL // (2 * PEER_SEL_TILES)
    sub8 = lax.broadcasted_iota(jnp.int32, (SUBLANES, LANES), 0)

    def fold(x, y, h):
        low = (sub8 % (2 * h)) < h
        if 2 * h == SUBLANES:
            return jnp.where(low, x, y) + pltpu.roll(jnp.where(low, y, x), h, 0)
        return jnp.where(low, x + pltpu.roll(x, SUBLANES - h, 0), y + pltpu.roll(y, h, 0))

    fold_order = (0, 4, 2, 6, 1, 5, 3, 7)

    def chunk_sums(tiles):
        t = [tiles[k] for k in fold_order]
        t = [fold(t[0], t[1], 4), fold(t[2], t[3], 4), fold(t[4], t[5], 4), fold(t[6], t[7], 4)]
        t = [fold(t[0], t[1], 2), fold(t[2], t[3], 2)]
        return fold(t[0], t[1], 1)

    def compute_group(g, s):
        rows = pl.ds(pl.multiple_of(g * G, G), G)
        gate_t = gate_ref[rows, :].T
        h2g = h2_ref[rows, :]
        sub = lax.broadcasted_iota(jnp.int32, (G, D_MODEL), 0)
        y = jnp.zeros((G, D_MODEL), F32)
        for r in range(G):
            tok = g * G + r
            h2t = jnp.zeros((SUBLANES, LANES), F32)
            for c in range(PEER_CHUNKS):
                h2t = jnp.where(sub8 == c, h2g[r:r + 1, c * LANES:(c + 1) * LANES], h2t)
            part = []
            for a in range(PEER_SEL_TILES):
                issue_rows(eidn_ref, g, 1 - s, tok, a * per_step, (a + 1) * per_step)
                part.append(chunk_sums([
                    lax.bitcast_convert_type(wbuf[s, tok, a * SUBLANES + b] & jnp.uint32(U_MASK), F32) * h2t
                    for b in range(SUBLANES)]))
            act = jnp.sum(jnp.stack(part), axis=-1, keepdims=True)
            act = 0.5 * act * (1.0 + lax.erf(act * (1.0 / math.sqrt(2.0))))
            w = gate_t[:, r:r + 1].reshape(PEER_SEL_TILES, SUBLANES, 1) * act
            w_scr[...] = jnp.broadcast_to(w, (PEER_SEL_TILES, SUBLANES, LANES))
            acc = jnp.zeros((SUBLANES, LANES), F32)
            for a in range(PEER_SEL_TILES):
                issue_rows(eidn_ref, g, 1 - s, tok, (PEER_SEL_TILES + a) * per_step,
                           (PEER_SEL_TILES + a + 1) * per_step)
                for b in range(SUBLANES):
                    vf = lax.bitcast_convert_type(wbuf[s, tok, a * SUBLANES + b] << 16, F32)
                    acc = acc + w_scr[a, b:b + 1, :] * vf
            yrow = jnp.concatenate([acc[c:c + 1, :] for c in range(PEER_CHUNKS)], axis=-1)
            y = jnp.where(sub == r, yrow, y)
        x2 = x1_ref[rows, :] + mod_ref[5:6, :] * y
        o_ref[rows, :] = x2 * lax.rsqrt(jnp.mean(x2 * x2, axis=-1, keepdims=True) + EPS) * gf_ref[...]

    @pl.when(i == 0)
    def _():
        def first(t, carry):
            issue_rows(eid_ref, t // G, 0, t, 0, PEER_SEL)
            return carry
        lax.fori_loop(0, PEER_TT, first, 0)

    def step(g, carry):
        wait_group(g, slot)
        compute_group(g, slot)
        return carry

    lax.fori_loop(0, PEER_NGROUP, step, 0)

    @pl.when(i == n - 1)
    def _():
        def drain(g, carry):
            wait_group(g, 1 - slot)
            return carry
        lax.fori_loop(0, PEER_NGROUP, drain, 0)


def _peer(eid_t, gate_t, h2, x1, mod3, g_final, table, seq):
    n_tok = h2.shape[0]
    tt = PEER_TT
    n = n_tok // tt
    smem_idx = lambda f: pl.BlockSpec((tt, PEER_SEL), f, memory_space=pltpu.SMEM)
    tok = lambda w: pl.BlockSpec((tt, w), lambda i: (i, 0))
    return pl.pallas_call(
        _peer_kernel,
        grid=(n,),
        in_specs=[
            smem_idx(lambda i: (i, 0)),
            smem_idx(lambda i: (jnp.minimum(i + 1, n - 1), 0)),
            tok(PEER_SEL), tok(D_MODEL), tok(D_MODEL),
            pl.BlockSpec((None, 6, D_MODEL), lambda i: ((i * tt) // seq, 0, 0)),
            pl.BlockSpec((1, D_MODEL), lambda i: (0, 0)),
            pl.BlockSpec(memory_space=pl.ANY),
        ],
        out_specs=tok(D_MODEL),
        out_shape=jax.ShapeDtypeStruct((n_tok, D_MODEL), F32),
        scratch_shapes=[pltpu.VMEM((2, tt, PEER_SEL, PEER_CHUNKS, LANES), jnp.uint32),
                        pltpu.SemaphoreType.DMA((2, PEER_NGROUP)),
                        pltpu.VMEM((PEER_SEL_TILES, SUBLANES, LANES), F32)],
        compiler_params=_cparams(("arbitrary",)),
    )(eid_t, eid_t, gate_t, h2, x1, mod3, g_final, table)


def _run_group(x, c, w_mod, b_mod, g1, w_in_bf, sink, p_lb_fwd, p_lb_bwd, g_hnorm, woa, woh, wout, g2,
               wpq, keys2d, table, g_final):
    B, S, _ = x.shape
    mod3 = _modulation(c, w_mod, b_mod).reshape(B, 6, D_MODEL)
    proj = _inproj(x, mod3, g1, w_in_bf)
    att = _attention(proj, sink)
    hg = _hgrn(proj, p_lb_fwd, p_lb_bwd, g_hnorm)
    x1, h2, qp = _outproj(att, hg, proj, x, mod3, woa, woh, wout, g2, wpq)
    n_tok = B * S
    eid, gate = _route(qp.reshape(n_tok, -1), keys2d)
    y = _peer(eid.T, gate.T, h2.reshape(n_tok, D_MODEL), x1.reshape(n_tok, D_MODEL), mod3, g_final, table, S)
    return y.reshape(B, S, D_MODEL)


def kernel(x_prompt, x_sample, c_prompt, c_sample, w_mod, b_mod, g_norm1, w_in, att_sink, p_lb_fwd, p_lb_bwd,
           g_hnorm, w_o_att, w_o_hgrn, w_out, g_norm2, w_pq, peer_keys, peer_u, peer_v, g_final):
    assert w_mod.shape[0] == 1 and p_lb_fwd.shape[0] == 2, "single-layer encoder"
    shared = (
        w_mod[0], b_mod[0], g_norm1[0].reshape(1, D_MODEL), w_in[0].astype(BF16), att_sink[0],
        p_lb_fwd, p_lb_bwd, g_hnorm[0].reshape(1, HG_V),
        w_o_att[0].astype(BF16), w_o_hgrn[0].astype(BF16), w_out[0].astype(BF16),
        g_norm2[0].reshape(1, D_MODEL), w_pq[0].astype(BF16),
        peer_keys[0].reshape(PEER_HEADS * 2 * PEER_NKEYS, PEER_HALF),
        _pack_tables(peer_u[0], peer_v[0]), g_final.reshape(1, D_MODEL),
    )
    return (_run_group(x_prompt, c_prompt, *shared), _run_group(x_sample, c_sample, *shared))
```

```python
import functools
import math

import jax
import jax.numpy as jnp
from jax import lax
from jax.experimental import pallas as pl
from jax.experimental.pallas import tpu as pltpu
from jax.experimental.pallas import tpu_sc as plsc

F32 = jnp.float32
BF16 = jnp.bfloat16
HIGHEST = lax.Precision.HIGHEST

D_MODEL = 1024
ATT_HEADS = 16
ATT_KV_HEADS = 4
ATT_HEAD_DIM = 64
ATT_GROUP = ATT_HEADS // ATT_KV_HEADS
WINDOW = 128
ATT_Q = ATT_HEADS * ATT_HEAD_DIM
ATT_KV = ATT_KV_HEADS * ATT_HEAD_DIM
HGRN_HEADS = 8
HGRN_DK = 128
HG_K = HGRN_HEADS * HGRN_DK
HG_V = HG_K
IN_COLS = ATT_Q + 2 * ATT_KV + 3 * HG_K + 2 * HG_V + 2 * D_MODEL
PEER_HEADS = 8
PEER_NKEYS = 128
PEER_HALF = 128
PEER_TOPK = 16
PEER_SEL = PEER_HEADS * PEER_TOPK
EPS = 1e-6

COL_QA = 0
COL_KA = COL_QA + ATT_Q
COL_VA = COL_KA + ATT_KV
COL_QH = COL_VA + ATT_KV
COL_ZF = COL_QH + HG_K
COL_ZB = COL_ZF + HG_K
COL_IH = COL_ZB + HG_K
COL_GH = COL_IH + HG_V
COL_GA = COL_GH + HG_V
COL_GB = COL_GA + D_MODEL

LANES = 128
SUBLANES = 8
VMEM_LIMIT = 56 * 1024 * 1024

HG_CHUNK = 128
HG_SUB = SUBLANES
HG_LEVELS = (64, 32, 16, 8)
HG_HEADS_PER_STEP = 1


def _cparams(sem):
    return pltpu.CompilerParams(dimension_semantics=sem, vmem_limit_bytes=VMEM_LIMIT)


def _nt_dot(a, b):
    return lax.dot_general(a, b, (((1,), (1,)), ((), ())), preferred_element_type=F32)


def _tn_dot(a, b):
    return lax.dot_general(a, b, (((0,), (0,)), ((), ())), preferred_element_type=F32)


def _mod_kernel(c_ref, w_ref, b_ref, o_ref):
    c = c_ref[...]
    sc = c * jax.nn.sigmoid(c)
    o_ref[...] = jnp.dot(sc, w_ref[...], preferred_element_type=F32, precision=HIGHEST) + b_ref[...]


def _modulation(c, w_mod, b_mod):
    B = c.shape[0]
    n = w_mod.shape[1]
    tn = D_MODEL
    return pl.pallas_call(
        _mod_kernel,
        grid=(n // tn,),
        in_specs=[
            pl.BlockSpec((B, D_MODEL), lambda j: (0, 0)),
            pl.BlockSpec((D_MODEL, tn), lambda j: (0, j)),
            pl.BlockSpec((1, tn), lambda j: (0, j)),
        ],
        out_specs=pl.BlockSpec((B, tn), lambda j: (0, j)),
        out_shape=jax.ShapeDtypeStruct((B, n), F32),
        compiler_params=_cparams(("parallel",)),
    )(c, w_mod, b_mod.reshape(1, n))


def _inproj_kernel(x_ref, mod_ref, g_ref, w_ref, o_ref, h_scr):
    @pl.when(pl.program_id(2) == 0)
    def _():
        x = x_ref[...]
        y = x * lax.rsqrt(jnp.mean(x * x, axis=-1, keepdims=True) + EPS) * g_ref[...]
        h = y * (1.0 + mod_ref[1:2, :]) + mod_ref[0:1, :]
        h_scr[...] = h.astype(BF16)

    o_ref[...] = jnp.dot(h_scr[...], w_ref[...], preferred_element_type=F32)


def _inproj(x, mod3, g1, w_in_bf):
    B, S, _ = x.shape
    tm = min(S, 1024)
    tn = 2176
    return pl.pallas_call(
        _inproj_kernel,
        grid=(B, S // tm, IN_COLS // tn),
        in_specs=[
            pl.BlockSpec((None, tm, D_MODEL), lambda b, i, j: (b, i, 0)),
            pl.BlockSpec((None, 6, D_MODEL), lambda b, i, j: (b, 0, 0)),
            pl.BlockSpec((1, D_MODEL), lambda b, i, j: (0, 0)),
            pl.BlockSpec((D_MODEL, tn), lambda b, i, j: (0, j)),
        ],
        out_specs=pl.BlockSpec((None, tm, tn), lambda b, i, j: (b, i, j)),
        out_shape=jax.ShapeDtypeStruct((B, S, IN_COLS), F32),
        scratch_shapes=[pltpu.VMEM((tm, D_MODEL), BF16)],
        compiler_params=_cparams(("parallel", "parallel", "arbitrary")),
    )(x, mod3, g1, w_in_bf)


def _attn_kernel(sink_ref, q_ref, kp_ref, kc_ref, kn_ref, vp_ref, vc_ref, vn_ref, o_ref, s_scr, p_scr, r_scr,
                 *, nblk):
    j = pl.program_id(1)
    blk = WINDOW
    span = 3 * blk
    scale = 1.0 / math.sqrt(ATT_HEAD_DIM)
    q = (q_ref[...] * scale).astype(BF16)
    k = jnp.concatenate([kp_ref[...], kc_ref[...], kn_ref[...]], axis=0).astype(BF16)
    v = jnp.concatenate([vp_ref[...], vc_ref[...], vn_ref[...]], axis=0).astype(BF16)
    qi = lax.broadcasted_iota(jnp.int32, (blk, span), 0)
    kj = lax.broadcasted_iota(jnp.int32, (blk, span), 1)
    dist = jnp.abs(kj - blk - qi)
    lo = jnp.where(j == 0, blk, 0)
    hi = jnp.where(j == nblk - 1, 2 * blk, span)
    valid = (dist <= WINDOW) & (kj >= lo) & (kj < hi)
    distf = dist.astype(F32)
    for hh in range(ATT_HEADS):
        h = hh // ATT_GROUP
        slope = 2.0 ** (-8.0 * (hh + 1) / ATT_HEADS)
        qh = q[:, hh * ATT_HEAD_DIM:(hh + 1) * ATT_HEAD_DIM]
        kh = k[:, h * ATT_HEAD_DIM:(h + 1) * ATT_HEAD_DIM]
        s = _nt_dot(qh, kh) - slope * distf
        s_scr[hh] = jnp.where(valid, s, -jnp.inf)
    for hh in range(ATT_HEADS):
        s = s_scr[hh]
        sink = sink_ref[hh]
        m = jnp.maximum(jnp.max(s, axis=-1, keepdims=True), sink)
        p = jnp.exp(s - m)
        r_scr[hh] = 1.0 / (jnp.sum(p, axis=-1, keepdims=True) + jnp.exp(sink - m))
        p_scr[hh] = p.astype(BF16)
    outs = []
    for hh in range(ATT_HEADS):
        h = hh // ATT_GROUP
        vh = v[:, h * ATT_HEAD_DIM:(h + 1) * ATT_HEAD_DIM]
        outs.append(jnp.dot(p_scr[hh], vh, preferred_element_type=F32) * r_scr[hh])
    o_ref[...] = jnp.concatenate(outs, axis=-1).astype(o_ref.dtype)


def _attention(proj, sink):
    B, S, _ = proj.shape
    blk = WINDOW
    nblk = S // blk
    kcol = COL_KA // ATT_KV
    vcol = COL_VA // ATT_KV

    def kv_spec(col, off):
        return pl.BlockSpec(
            (None, blk, ATT_KV),
            lambda b, j: (b, jnp.clip(j + off, 0, nblk - 1), col))

    return pl.pallas_call(
        functools.partial(_attn_kernel, nblk=nblk),
        grid=(B, nblk),
        in_specs=[
            pl.BlockSpec(memory_space=pltpu.SMEM),
            pl.BlockSpec((None, blk, ATT_Q), lambda b, j: (b, j, 0)),
            kv_spec(kcol, -1), kv_spec(kcol, 0), kv_spec(kcol, 1),
            kv_spec(vcol, -1), kv_spec(vcol, 0), kv_spec(vcol, 1),
        ],
        out_specs=pl.BlockSpec((None, blk, ATT_Q), lambda b, j: (b, j, 0)),
        out_shape=jax.ShapeDtypeStruct((B, S, ATT_Q), BF16),
        scratch_shapes=[pltpu.VMEM((ATT_HEADS, blk, 3 * blk), F32),
                        pltpu.VMEM((ATT_HEADS, blk, 3 * blk), BF16),
                        pltpu.VMEM((ATT_HEADS, blk, 1), F32)],
        compiler_params=_cparams(("parallel", "parallel")),
    )(sink, proj, proj, proj, proj, proj, proj, proj)


def _hgrn_chunk(q_raw, z, v, lb, st_scr, reverse):
    C = HG_CHUNK
    q = q_raw * jax.nn.sigmoid(q_raw)
    f = lb + (1.0 - lb) * jax.nn.sigmoid(z)
    kk = (1.0 - lb) * jax.nn.sigmoid(-z)
    logf = jnp.log(f) * (1.0 / math.log(2.0))
    row = lax.broadcasted_iota(jnp.int32, (C, C), 0)
    col = lax.broadcasted_iota(jnp.int32, (C, C), 1)
    tri = jnp.where((row <= col) if reverse else (row >= col), 1.0, 0.0).astype(F32)
    b = jnp.dot(tri, logf, preferred_element_type=F32, precision=HIGHEST)
    edge = b[0:1, :] if reverse else b[C - 1:C, :]
    vb = v.astype(BF16)

    st = st_scr[...]
    o = _nt_dot((q * jnp.exp2(b)).astype(BF16), st.astype(BF16))
    kh = (kk * jnp.exp2(edge - b)).astype(BF16)
    st_scr[...] = st * jnp.exp2(edge) + _tn_dot(vb, kh)

    pos = lax.broadcasted_iota(jnp.int32, (C, 1), 0)
    a = jnp.zeros((C, C), F32)
    for hsz in HG_LEVELS:
        seg = 2 * hsz
        refs = []
        for s0 in range(0, C, seg):
            r = s0 + hsz if reverse else s0 + hsz - 1
            refs.append(jnp.broadcast_to(b[r:r + 1, :], (seg, HGRN_DK)))
        bmid = jnp.concatenate(refs, axis=0) if len(refs) > 1 else refs[0]
        e = jnp.exp2(-jnp.abs(b - bmid))
        upper = (pos % seg) >= hsz
        qrole = jnp.logical_not(upper) if reverse else upper
        ql = jnp.where(qrole, q * e, 0.0).astype(BF16)
        kl = jnp.where(qrole, 0.0, kk * e).astype(BF16)
        al = _nt_dot(ql, kl)
        same = (row // seg) == (col // seg)
        rq = (row % seg) >= hsz
        ck = (col % seg) >= hsz
        if reverse:
            m = same & jnp.logical_not(rq) & ck
        else:
            m = same & rq & jnp.logical_not(ck)
        a = jnp.where(m, al, a)
    nb = C // HG_SUB
    shape3 = (nb, HG_SUB, HGRN_DK)
    b3 = b.reshape(shape3)
    q3 = q.reshape(shape3)
    k3 = kk.reshape(shape3)
    sub = lax.broadcasted_iota(jnp.int32, shape3, 1)
    block_col0 = (row // HG_SUB) * HG_SUB
    for s in range(HG_SUB):
        keep = (sub <= s) if reverse else (sub >= s)
        arg = jnp.where(keep, b3 - b3[:, s:s + 1, :], -jnp.inf)
        g = q3 * k3[:, s:s + 1, :] * jnp.exp2(arg)
        w = jnp.sum(g, axis=-1, keepdims=True).reshape(C, 1)
        a = jnp.where(col == block_col0 + s, w, a)
    return o + jnp.dot(a.astype(BF16), vb, preferred_element_type=F32)


def _lower_bound(plb_ref):
    p = plb_ref[...]
    e = jnp.exp(p - jnp.max(p, axis=0, keepdims=True))
    return e[0:1, :] / jnp.sum(e, axis=0, keepdims=True)


def _hgrn_fwd_kernel(q_ref, z_ref, v_ref, plb_ref, o_ref, st_scr, *, nchunk):
    @pl.when(pl.program_id(2) == 0)
    def _():
        st_scr[...] = jnp.zeros_like(st_scr)

    lb = _lower_bound(plb_ref)
    C = HG_CHUNK
    for ci in range(nchunk):
        sl = slice(ci * C, (ci + 1) * C)
        for hh in range(HG_HEADS_PER_STEP):
            cs = slice(hh * HGRN_DK, (hh + 1) * HGRN_DK)
            o_ref[sl, cs] = _hgrn_chunk(q_ref[sl, cs], z_ref[sl, cs], v_ref[sl, cs], lb[:, cs],
                                        st_scr.at[hh], False)


def _hgrn_bwd_kernel(q_ref, z_ref, v_ref, plb_ref, of_ref, gh_ref, gn_ref, o_ref, st_scr, *, nchunk):
    @pl.when(pl.program_id(2) == 0)
    def _():
        st_scr[...] = jnp.zeros_like(st_scr)

    lb = _lower_bound(plb_ref)
    C = HG_CHUNK
    for ci in reversed(range(nchunk)):
        sl = slice(ci * C, (ci + 1) * C)
        for hh in range(HG_HEADS_PER_STEP):
            cs = slice(hh * HGRN_DK, (hh + 1) * HGRN_DK)
            o = _hgrn_chunk(q_ref[sl, cs], z_ref[sl, cs], v_ref[sl, cs], lb[:, cs], st_scr.at[hh], True)
            o = o + of_ref[sl, cs]
            o = o * lax.rsqrt(jnp.mean(o * o, axis=-1, keepdims=True) + EPS) * gn_ref[:, cs]
            gh = gh_ref[sl, cs]
            o_ref[sl, cs] = (o * (gh * jax.nn.sigmoid(gh))).astype(o_ref.dtype)


def _hgrn(proj, p_lb_fwd, p_lb_bwd, g_hnorm):
    B, S, _ = proj.shape
    tb = min(S, 1024)
    nt = S // tb
    nchunk = tb // HG_CHUNK
    hp = HG_HEADS_PER_STEP
    dk = HGRN_DK
    wd = hp * dk
    scratch = [pltpu.VMEM((hp, dk, dk), F32)]

    def col_spec(col0, rev):
        if rev:
            return pl.BlockSpec((None, tb, wd), lambda b, h, t: (b, nt - 1 - t, col0 // wd + h))
        return pl.BlockSpec((None, tb, wd), lambda b, h, t: (b, t, col0 // wd + h))

    plb_spec = pl.BlockSpec((p_lb_fwd.shape[0], wd), lambda b, h, t: (0, h))
    sem = ("parallel", "parallel", "arbitrary")
    o_f = pl.pallas_call(
        functools.partial(_hgrn_fwd_kernel, nchunk=nchunk),
        grid=(B, HGRN_HEADS // hp, nt),
        in_specs=[col_spec(COL_QH, False), col_spec(COL_ZF, False), col_spec(COL_IH, False), plb_spec],
        out_specs=pl.BlockSpec((None, tb, wd), lambda b, h, t: (b, t, h)),
        out_shape=jax.ShapeDtypeStruct((B, S, HG_V), F32),
        scratch_shapes=scratch,
        compiler_params=_cparams(sem),
    )(proj, proj, proj, p_lb_fwd)
    return pl.pallas_call(
        functools.partial(_hgrn_bwd_kernel, nchunk=nchunk),
        grid=(B, HGRN_HEADS // hp, nt),
        in_specs=[col_spec(COL_QH, True), col_spec(COL_ZB, True), col_spec(COL_IH, True), plb_spec,
                  pl.BlockSpec((None, tb, wd), lambda b, h, t: (b, nt - 1 - t, h)),
                  col_spec(COL_GH, True),
                  pl.BlockSpec((1, wd), lambda b, h, t: (0, h))],
        out_specs=pl.BlockSpec((None, tb, wd), lambda b, h, t: (b, nt - 1 - t, h)),
        out_shape=jax.ShapeDtypeStruct((B, S, HG_V), BF16),
        scratch_shapes=scratch,
        compiler_params=_cparams(sem),
    )(proj, proj, proj, p_lb_bwd, o_f, proj, g_hnorm)


def _outproj_kernel(att_ref, hg_ref, ga0_ref, ga1_ref, gb0_ref, gb1_ref, x_ref, mod_ref, woa_ref, woh_ref,
                    wout_ref, g2_ref, wpq_ref, x1_ref, h2_ref, qp_ref):
    ya = jnp.dot(att_ref[...], woa_ref[...], preferred_element_type=F32)
    yh = jnp.dot(hg_ref[...], woh_ref[...], preferred_element_type=F32)
    ga = jnp.concatenate([ga0_ref[...], ga1_ref[...]], axis=-1)
    gb = jnp.concatenate([gb0_ref[...], gb1_ref[...]], axis=-1)
    merged = jax.nn.sigmoid(ga) * ya + jax.nn.sigmoid(gb) * yh
    x1 = x_ref[...] + mod_ref[2:3, :] * jnp.dot(merged.astype(BF16), wout_ref[...],
                                                preferred_element_type=F32)
    x1_ref[...] = x1
    y = x1 * lax.rsqrt(jnp.mean(x1 * x1, axis=-1, keepdims=True) + EPS) * g2_ref[...]
    h2 = y * (1.0 + mod_ref[4:5, :]) + mod_ref[3:4, :]
    h2_ref[...] = h2
    qp_ref[...] = jnp.dot(h2.astype(BF16), wpq_ref[...], preferred_element_type=F32)


def _outproj(att, hg, proj, x, mod3, woa, woh, wout, g2, wpq):
    B, S, _ = x.shape
    tm = 256
    half = D_MODEL // 2
    nq = wpq.shape[1]

    def tok(width, col):
        return pl.BlockSpec((None, tm, width), lambda b, i: (b, i, col))

    def full(shape):
        return pl.BlockSpec(shape, lambda b, i: (0, 0))

    return pl.pallas_call(
        _outproj_kernel,
        grid=(B, S // tm),
        in_specs=[
            tok(ATT_Q, 0), tok(HG_V, 0),
            tok(half, COL_GA // half), tok(half, COL_GA // half + 1),
            tok(half, COL_GB // half), tok(half, COL_GB // half + 1),
            tok(D_MODEL, 0),
            pl.BlockSpec((None, 6, D_MODEL), lambda b, i: (b, 0, 0)),
            full((ATT_Q, D_MODEL)), full((HG_V, D_MODEL)), full((D_MODEL, D_MODEL)),
            full((1, D_MODEL)), full((D_MODEL, nq)),
        ],
        out_specs=[tok(D_MODEL, 0), tok(D_MODEL, 0), tok(nq, 0)],
        out_shape=[jax.ShapeDtypeStruct((B, S, D_MODEL), F32),
                   jax.ShapeDtypeStruct((B, S, D_MODEL), F32),
                   jax.ShapeDtypeStruct((B, S, nq), F32)],
        compiler_params=_cparams(("parallel", "parallel")),
    )(att, hg, proj, proj, proj, proj, x, mod3, woa, woh, wout, g2, wpq)


ROUTE_HEADS = 4
ROUTE_TT = 256


def _route_kernel(q_ref, keys_ref, eid_ref, gate_ref, sv_scr, si_scr, cand_scr, cid_scr, ts_scr):
    tt = q_ref.shape[0]
    n = PEER_NKEYS
    k = PEER_TOPK
    nch = 2 * ROUTE_HEADS
    q = q_ref[...].astype(BF16)
    iota_n = lax.broadcasted_iota(jnp.int32, (n, tt), 0).astype(F32)
    s = [_nt_dot(keys_ref[c * n:(c + 1) * n, :].astype(BF16), q[:, c * PEER_HALF:(c + 1) * PEER_HALF])
         for c in range(nch)]
    for i in range(k):
        for c in range(nch):
            m = jnp.max(s[c], axis=0, keepdims=True)
            tagged = jnp.where(s[c] == m, iota_n, float(n))
            idx = jnp.min(tagged, axis=0, keepdims=True)
            sv_scr[c, i:i + 1, :] = m
            si_scr[c, i:i + 1, :] = idx
            s[c] = jnp.where(tagged == idx, -jnp.inf, s[c])
    ncand = cand_scr.shape[1]
    cand_scr[...] = jnp.full(cand_scr.shape, -jnp.inf, F32)
    cid_scr[...] = jnp.zeros(cid_scr.shape, F32)
    for h in range(ROUTE_HEADS):
        off = 0
        for a in range(k):
            nb = k // (a + 1)
            cand_scr[h, off:off + nb, :] = sv_scr[2 * h, a:a + 1, :] + sv_scr[2 * h + 1, 0:nb, :]
            cid_scr[h, off:off + nb, :] = si_scr[2 * h, a:a + 1, :] * float(n) + si_scr[2 * h + 1, 0:nb, :]
            off += nb
    cs = [cand_scr[h] for h in range(ROUTE_HEADS)]
    ids = [cid_scr[h] for h in range(ROUTE_HEADS)]
    iota_c = lax.broadcasted_iota(jnp.int32, (ncand, tt), 0).astype(F32)
    for i in range(k):
        for h in range(ROUTE_HEADS):
            m = jnp.max(cs[h], axis=0, keepdims=True)
            tagged = jnp.where(cs[h] == m, iota_c, float(ncand))
            sel = tagged == jnp.min(tagged, axis=0, keepdims=True)
            r = h * k + i
            eid_ref[r:r + 1, :] = jnp.max(jnp.where(sel, ids[h], -1.0), axis=0, keepdims=True).astype(jnp.int32)
            ts_scr[r:r + 1, :] = m
            cs[h] = jnp.where(sel, -jnp.inf, cs[h])
    for h in range(ROUTE_HEADS):
        ts = ts_scr[h * k:(h + 1) * k, :]
        ex = jnp.exp(ts - ts[0:1, :])
        gate_ref[h * k:(h + 1) * k, :] = ex / jnp.sum(ex, axis=0, keepdims=True)


def _route(qp, keys2d):
    n_tok = qp.shape[0]
    tt = ROUTE_TT
    hp = ROUTE_HEADS
    k = PEER_TOPK
    npairs = sum(k // (a + 1) for a in range(k))
    ncand = -(-npairs // SUBLANES) * SUBLANES
    return pl.pallas_call(
        _route_kernel,
        grid=(n_tok // tt, PEER_HEADS // hp),
        in_specs=[
            pl.BlockSpec((tt, hp * 2 * PEER_HALF), lambda i, h: (i, h)),
            pl.BlockSpec((hp * 2 * PEER_NKEYS, PEER_HALF), lambda i, h: (h, 0)),
        ],
        out_specs=[pl.BlockSpec((hp * k, tt), lambda i, h: (h, i)),
                   pl.BlockSpec((hp * k, tt), lambda i, h: (h, i))],
        out_shape=[jax.ShapeDtypeStruct((PEER_SEL, n_tok), jnp.int32),
                   jax.ShapeDtypeStruct((PEER_SEL, n_tok), F32)],
        scratch_shapes=[pltpu.VMEM((2 * hp, k, tt), F32), pltpu.VMEM((2 * hp, k, tt), F32),
                        pltpu.VMEM((hp, ncand, tt), F32), pltpu.VMEM((hp, ncand, tt), F32),
                        pltpu.VMEM((hp * k, tt), F32)],
        compiler_params=_cparams(("parallel", "arbitrary")),
    )(qp, keys2d)


PEER_GROUP = SUBLANES
PEER_STEP_GROUPS = 4
PEER_SPLIT_FIRST = (4, 0)
PEER_SPLIT_SECOND = (2, 2)
U_MASK = 0xFFFF0000
PEER_CHUNKS = D_MODEL // LANES
PEER_SEL_TILES = PEER_SEL // SUBLANES
PEER_SUBSTEPS = 2 * PEER_SEL_TILES
SC_WINDOW = 128


def _pack_tables(u, v):
    ub = lax.bitcast_convert_type(u.astype(BF16), jnp.uint16).astype(jnp.uint32)
    vb = lax.bitcast_convert_type(v.astype(BF16), jnp.uint16).astype(jnp.uint32)
    return ((ub << 16) | vb).reshape(u.shape[0], PEER_CHUNKS, LANES)


def _sc_gather(table_rows, idx):
    n = idx.shape[0]
    mesh = plsc.VectorSubcoreMesh(core_axis_name="core", subcore_axis_name="subcore")

    @pl.kernel(out_type=jax.ShapeDtypeStruct((n, LANES), table_rows.dtype), mesh=mesh)
    def gather(x_hbm, i_hbm, o_hbm):
        def body(i_vmem, o_vmem):
            pltpu.sync_copy(x_hbm.at[i_vmem.at[0]], o_vmem)

        pltpu.emit_pipeline(
            body,
            grid=(n // SC_WINDOW,),
            in_specs=[pl.BlockSpec((1, SC_WINDOW), index_map=lambda i: (0, i))],
            out_specs=[pl.BlockSpec((SC_WINDOW, LANES), index_map=lambda i: (i, 0))],
            core_axis_name=("core", "subcore"),
            dimension_semantics=(pltpu.PARALLEL,),
        )(i_hbm, o_hbm)

    return gather(table_rows, idx.reshape(1, n))


def _peer_kernel(*refs, seq, dg, sg):
    if sg:
        (eid_ref, eidn_ref, gate_ref, h2_ref, x1_ref, mod_ref, modl_ref, sgate_ref, sh2_ref, sx1_ref, smod_ref,
         staged_ref, gf_ref, tab_hbm, o_ref, so_ref, wbuf, sem, w_scr) = refs
    else:
        eid_ref, eidn_ref, gate_ref, h2_ref, x1_ref, mod_ref, modl_ref, gf_ref, tab_hbm, o_ref, wbuf, sem, w_scr = refs
    i = pl.program_id(0)
    n = pl.num_programs(0)
    slot = i % 2
    G = PEER_GROUP
    td = dg * G
    groups = dg + sg
    bundle_tokens = td // groups
    n_issue = td * PEER_SEL // (groups * G * PEER_SUBSTEPS)

    def issue_rows(idx_ref, s, q, first_tok, k0, k1):
        for k in range(k0, k1):
            tok = first_tok + k // PEER_SEL
            j = k % PEER_SEL
            e = idx_ref[tok, j]
            pltpu.make_async_copy(tab_hbm.at[e], wbuf.at[s, tok, j], sem.at[s, q]).start(priority=j % 2)

    def wait_bundle(q, s):
        toks = pl.ds(q * bundle_tokens, bundle_tokens)
        pltpu.make_async_copy(wbuf.at[s, toks], wbuf.at[s, toks], sem.at[s, q]).wait()

    sub8 = lax.broadcasted_iota(jnp.int32, (SUBLANES, LANES), 0)

    def fold(x, y, h):
        low = (sub8 % (2 * h)) < h
        if 2 * h == SUBLANES:
            return jnp.where(low, x, y) + pltpu.roll(jnp.where(low, y, x), h, 0)
        return jnp.where(low, x + pltpu.roll(x, SUBLANES - h, 0), y + pltpu.roll(y, h, 0))

    fold_order = (0, 4, 2, 6, 1, 5, 3, 7)

    def chunk_sums(tiles):
        t = [tiles[k] for k in fold_order]
        t = [fold(t[0], t[1], 4), fold(t[2], t[3], 4), fold(t[4], t[5], 4), fold(t[6], t[7], 4)]
        t = [fold(t[0], t[1], 2), fold(t[2], t[3], 2)]
        return fold(t[0], t[1], 1)

    def compute_group(load, gate_rows, h2g, x1g, gate2, out_ref, out_rows, q):
        gate_t = gate_rows.T
        sub = lax.broadcasted_iota(jnp.int32, (G, D_MODEL), 0)
        y = jnp.zeros((G, D_MODEL), F32)
        first_tok = q * bundle_tokens
        for r in range(G):
            k_base = r * PEER_SUBSTEPS * n_issue
            h2t = jnp.zeros((SUBLANES, LANES), F32)
            for c in range(PEER_CHUNKS):
                h2t = jnp.where(sub8 == c, h2g[r:r + 1, c * LANES:(c + 1) * LANES], h2t)
            part = []
            for a in range(PEER_SEL_TILES):
                issue_rows(eidn_ref, 1 - slot, q, first_tok, k_base + a * n_issue,
                           k_base + (a + 1) * n_issue)
                part.append(chunk_sums([
                    lax.bitcast_convert_type(load(r, a * SUBLANES + b) & jnp.uint32(U_MASK), F32) * h2t
                    for b in range(SUBLANES)]))
            act = jnp.sum(jnp.stack(part), axis=-1, keepdims=True)
            act = 0.5 * act * (1.0 + lax.erf(act * (1.0 / math.sqrt(2.0))))
            w = gate_t[:, r:r + 1].reshape(PEER_SEL_TILES, SUBLANES, 1) * act
            w_scr[...] = jnp.broadcast_to(w, (PEER_SEL_TILES, SUBLANES, LANES))
            acc = jnp.zeros((SUBLANES, LANES), F32)
            for a in range(PEER_SEL_TILES):
                issue_rows(eidn_ref, 1 - slot, q, first_tok, k_base + (PEER_SEL_TILES + a) * n_issue,
                           k_base + (PEER_SEL_TILES + a + 1) * n_issue)
                for b in range(SUBLANES):
                    vf = lax.bitcast_convert_type(load(r, a * SUBLANES + b) << 16, F32)
                    acc = acc + w_scr[a, b:b + 1, :] * vf
            yrow = jnp.concatenate([acc[c:c + 1, :] for c in range(PEER_CHUNKS)], axis=-1)
            y = jnp.where(sub == r, yrow, y)
        x2 = x1g + gate2 * y
        out_ref[out_rows, :] = x2 * lax.rsqrt(jnp.mean(x2 * x2, axis=-1, keepdims=True) + EPS) * gf_ref[...]

    @pl.when(i == 0)
    def _():
        def first(t, carry):
            issue_rows(eid_ref, 0, t // bundle_tokens, t, 0, PEER_SEL)
            return carry
        lax.fori_loop(0, td, first, 0)

    def direct(g, carry):
        last = (g * G + G - 1) // bundle_tokens
        prev = jnp.where(g == 0, -1, (g * G - 1) // bundle_tokens)
        for q in range(groups):
            @pl.when((q > prev) & (q <= last))
            def _():
                wait_bundle(q, slot)

        rows = pl.ds(pl.multiple_of(g * G, G), G)
        in_last = (i * td + g * G) // seq != (i * td) // seq
        gate2 = jnp.where(in_last, modl_ref[5:6, :], mod_ref[5:6, :])
        compute_group(lambda r, j: wbuf[slot, g * G + r, j], gate_ref[rows, :], h2_ref[rows, :],
                      x1_ref[rows, :], gate2, o_ref, rows, g)
        return carry

    lax.fori_loop(0, dg, direct, 0)

    def from_staged(g, carry):
        rows = pl.ds(pl.multiple_of(g * G, G), G)
        compute_group(lambda r, j: staged_ref[g * G + r, j], sgate_ref[rows, :], sh2_ref[rows, :],
                      sx1_ref[rows, :], smod_ref[5:6, :], so_ref, rows, dg + g)
        return carry

    if sg:
        lax.fori_loop(0, sg, from_staged, 0)

    @pl.when(i == n - 1)
    def _():
        def drain(q, carry):
            wait_bundle(q, 1 - slot)
            return carry
        lax.fori_loop(0, groups, drain, 0)


def _peer(eid_t, gate_t, h2, x1, mod3, g_final, table, staged, seq, split):
    dg, sg = split
    n_tok = h2.shape[0]
    G = PEER_GROUP
    td = dg * G
    ts = sg * G
    ns = staged.shape[0] if sg else 0
    nd = n_tok - ns
    n = nd // td
    assert nd % td == 0 and ns == n * ts and td % (dg + sg) == 0, (n_tok, ns, split)
    smem_idx = lambda f: pl.BlockSpec((td, PEER_SEL), f, memory_space=pltpu.SMEM)
    tok = lambda w: pl.BlockSpec((td, w), lambda i: (i, 0))
    in_specs = [
        smem_idx(lambda i: (i, 0)),
        smem_idx(lambda i: (jnp.minimum(i + 1, n - 1), 0)),
        tok(PEER_SEL), tok(D_MODEL), tok(D_MODEL),
        pl.BlockSpec((None, 6, D_MODEL), lambda i: ((i * td) // seq, 0, 0)),
        pl.BlockSpec((None, 6, D_MODEL), lambda i: ((i * td + td - 1) // seq, 0, 0)),
    ]
    operands = [eid_t, eid_t, gate_t, h2, x1, mod3, mod3]
    out_specs = [tok(D_MODEL)]
    out_shape = [jax.ShapeDtypeStruct((nd, D_MODEL), F32)]
    if sg:
        soff = nd // ts
        stok = lambda w: pl.BlockSpec((ts, w), lambda i: (soff + i, 0))
        in_specs += [
            stok(PEER_SEL), stok(D_MODEL), stok(D_MODEL),
            pl.BlockSpec((None, 6, D_MODEL), lambda i: ((nd + i * ts) // seq, 0, 0)),
            pl.BlockSpec((ts, PEER_SEL, PEER_CHUNKS, LANES), lambda i: (i, 0, 0, 0)),
        ]
        operands += [gate_t, h2, x1, mod3, staged]
        out_specs.append(pl.BlockSpec((ts, D_MODEL), lambda i: (i, 0)))
        out_shape.append(jax.ShapeDtypeStruct((ns, D_MODEL), F32))
    in_specs += [pl.BlockSpec((1, D_MODEL), lambda i: (0, 0)), pl.BlockSpec(memory_space=pl.ANY)]
    operands += [g_final, table]
    outs = pl.pallas_call(
        functools.partial(_peer_kernel, seq=seq, dg=dg, sg=sg),
        grid=(n,),
        in_specs=in_specs,
        out_specs=out_specs,
        out_shape=out_shape,
        scratch_shapes=[pltpu.VMEM((2, td, PEER_SEL, PEER_CHUNKS, LANES), jnp.uint32),
                        pltpu.SemaphoreType.DMA((2, dg + sg)),
                        pltpu.VMEM((PEER_SEL_TILES, SUBLANES, LANES), F32)],
        compiler_params=_cparams(("arbitrary",)),
    )(*operands)
    return jnp.concatenate(outs, axis=0) if sg else outs[0]


def _pre_peer(x, c, w_mod, b_mod, g1, w_in_bf, sink, p_lb_fwd, p_lb_bwd, g_hnorm, woa, woh, wout, g2, wpq,
              keys2d, table, split):
    B, S, _ = x.shape
    mod3 = _modulation(c, w_mod, b_mod).reshape(B, 6, D_MODEL)
    proj = _inproj(x, mod3, g1, w_in_bf)
    att = _attention(proj, sink)
    hg = _hgrn(proj, p_lb_fwd, p_lb_bwd, g_hnorm)
    x1, h2, qp = _outproj(att, hg, proj, x, mod3, woa, woh, wout, g2, wpq)
    n_tok = B * S
    eid, gate = _route(qp.reshape(n_tok, -1), keys2d)
    eid_t = eid.T
    staged = None
    if split[1]:
        ns = n_tok * split[1] // (split[0] + split[1])
        rows = eid_t[n_tok - ns:, :, None] * PEER_CHUNKS + jnp.arange(PEER_CHUNKS, dtype=jnp.int32)
        staged = _sc_gather(table.reshape(-1, LANES), rows.reshape(-1))
        staged = staged.reshape(ns, PEER_SEL, PEER_CHUNKS, LANES)
    return eid_t, gate.T, h2.reshape(n_tok, D_MODEL), x1.reshape(n_tok, D_MODEL), mod3, staged


def kernel(x_prompt, x_sample, c_prompt, c_sample, w_mod, b_mod, g_norm1, w_in, att_sink, p_lb_fwd, p_lb_bwd,
           g_hnorm, w_o_att, w_o_hgrn, w_out, g_norm2, w_pq, peer_keys, peer_u, peer_v, g_final):
    assert w_mod.shape[0] == 1 and p_lb_fwd.shape[0] == 2, "single-layer encoder"
    table = _pack_tables(peer_u[0], peer_v[0])
    gf = g_final.reshape(1, D_MODEL)
    shared = (
        w_mod[0], b_mod[0], g_norm1[0].reshape(1, D_MODEL), w_in[0].astype(BF16), att_sink[0],
        p_lb_fwd, p_lb_bwd, g_hnorm[0].reshape(1, HG_V),
        w_o_att[0].astype(BF16), w_o_hgrn[0].astype(BF16), w_out[0].astype(BF16),
        g_norm2[0].reshape(1, D_MODEL), w_pq[0].astype(BF16),
        peer_keys[0].reshape(PEER_HEADS * 2 * PEER_NKEYS, PEER_HALF), table,
    )
    first = _pre_peer(x_prompt, c_prompt, *shared, PEER_SPLIT_FIRST)
    second = _pre_peer(x_sample, c_sample, *shared, PEER_SPLIT_SECOND)
    y_first = _peer(*first[:5], gf, table, first[5], x_prompt.shape[1], PEER_SPLIT_FIRST)
    y_first, staged = lax.optimization_barrier((y_first, second[5]))
    y_second = _peer(*second[:5], gf, table, staged, x_sample.shape[1], PEER_SPLIT_SECOND)
    return (y_first.reshape(x_prompt.shape), y_second.reshape(x_sample.shape))
```

```python
import functools
import math

import jax
import jax.numpy as jnp
from jax import lax
from jax.experimental import pallas as pl
from jax.experimental.pallas import tpu as pltpu

F32 = jnp.float32
BF16 = jnp.bfloat16
HIGHEST = lax.Precision.HIGHEST

D_MODEL = 1024
ATT_HEADS = 16
ATT_KV_HEADS = 4
ATT_HEAD_DIM = 64
ATT_GROUP = ATT_HEADS // ATT_KV_HEADS
WINDOW = 128
ATT_Q = ATT_HEADS * ATT_HEAD_DIM
ATT_KV = ATT_KV_HEADS * ATT_HEAD_DIM
HGRN_HEADS = 8
HGRN_DK = 128
HG_K = HGRN_HEADS * HGRN_DK
HG_V = HG_K
IN_COLS = ATT_Q + 2 * ATT_KV + 3 * HG_K + 2 * HG_V + 2 * D_MODEL
PEER_HEADS = 8
PEER_NKEYS = 128
PEER_HALF = 128
PEER_TOPK = 16
PEER_SEL = PEER_HEADS * PEER_TOPK
EPS = 1e-6

COL_QA = 0
COL_KA = COL_QA + ATT_Q
COL_VA = COL_KA + ATT_KV
COL_QH = COL_VA + ATT_KV
COL_ZF = COL_QH + HG_K
COL_ZB = COL_ZF + HG_K
COL_IH = COL_ZB + HG_K
COL_GH = COL_IH + HG_V
COL_GA = COL_GH + HG_V
COL_GB = COL_GA + D_MODEL

LANES = 128
SUBLANES = 8
VMEM_LIMIT = 56 * 1024 * 1024

HG_CHUNK = 128
HG_LEVELS = (64, 32, 16, 8, 4, 2, 1)


def _cparams(sem):
    return pltpu.CompilerParams(dimension_semantics=sem, vmem_limit_bytes=VMEM_LIMIT)


def _nt_dot(a, b):
    return lax.dot_general(a, b, (((1,), (1,)), ((), ())), preferred_element_type=F32)


def _tn_dot(a, b):
    return lax.dot_general(a, b, (((0,), (0,)), ((), ())), preferred_element_type=F32)


def _mod_kernel(c_ref, w_ref, b_ref, o_ref):
    c = c_ref[...]
    sc = c * jax.nn.sigmoid(c)
    o_ref[...] = jnp.dot(sc, w_ref[...], preferred_element_type=F32, precision=HIGHEST) + b_ref[...]


def _modulation(c, w_mod, b_mod):
    B = c.shape[0]
    n = w_mod.shape[1]
    tn = D_MODEL
    return pl.pallas_call(
        _mod_kernel,
        grid=(n // tn,),
        in_specs=[
            pl.BlockSpec((B, D_MODEL), lambda j: (0, 0)),
            pl.BlockSpec((D_MODEL, tn), lambda j: (0, j)),
            pl.BlockSpec((1, tn), lambda j: (0, j)),
        ],
        out_specs=pl.BlockSpec((B, tn), lambda j: (0, j)),
        out_shape=jax.ShapeDtypeStruct((B, n), F32),
        compiler_params=_cparams(("parallel",)),
    )(c, w_mod, b_mod.reshape(1, n))


def _inproj_kernel(x_ref, mod_ref, g_ref, w_ref, o_ref, h_scr):
    @pl.when(pl.program_id(2) == 0)
    def _():
        x = x_ref[...]
        y = x * lax.rsqrt(jnp.mean(x * x, axis=-1, keepdims=True) + EPS) * g_ref[...]
        h = y * (1.0 + mod_ref[1:2, :]) + mod_ref[0:1, :]
        h_scr[...] = h.astype(BF16)

    o_ref[...] = jnp.dot(h_scr[...], w_ref[...], preferred_element_type=F32)


def _inproj(x, mod3, g1, w_in_bf):
    B, S, _ = x.shape
    tm = min(S, 1024)
    tn = 2176
    return pl.pallas_call(
        _inproj_kernel,
        grid=(B, S // tm, IN_COLS // tn),
        in_specs=[
            pl.BlockSpec((None, tm, D_MODEL), lambda b, i, j: (b, i, 0)),
            pl.BlockSpec((None, 6, D_MODEL), lambda b, i, j: (b, 0, 0)),
            pl.BlockSpec((1, D_MODEL), lambda b, i, j: (0, 0)),
            pl.BlockSpec((D_MODEL, tn), lambda b, i, j: (0, j)),
        ],
        out_specs=pl.BlockSpec((None, tm, tn), lambda b, i, j: (b, i, j)),
        out_shape=jax.ShapeDtypeStruct((B, S, IN_COLS), F32),
        scratch_shapes=[pltpu.VMEM((tm, D_MODEL), BF16)],
        compiler_params=_cparams(("parallel", "parallel", "arbitrary")),
    )(x, mod3, g1, w_in_bf)


def _attn_kernel(sink_ref, q_ref, kp_ref, kc_ref, kn_ref, vp_ref, vc_ref, vn_ref, o_ref, s_scr, p_scr, r_scr,
                 *, nblk):
    j = pl.program_id(1)
    blk = WINDOW
    span = 3 * blk
    scale = 1.0 / math.sqrt(ATT_HEAD_DIM)
    q = (q_ref[...] * scale).astype(BF16)
    k = jnp.concatenate([kp_ref[...], kc_ref[...], kn_ref[...]], axis=0).astype(BF16)
    v = jnp.concatenate([vp_ref[...], vc_ref[...], vn_ref[...]], axis=0).astype(BF16)
    qi = lax.broadcasted_iota(jnp.int32, (blk, span), 0)
    kj = lax.broadcasted_iota(jnp.int32, (blk, span), 1)
    dist = jnp.abs(kj - blk - qi)
    lo = jnp.where(j == 0, blk, 0)
    hi = jnp.where(j == nblk - 1, 2 * blk, span)
    valid = (dist <= WINDOW) & (kj >= lo) & (kj < hi)
    distf = dist.astype(F32)
    for hh in range(ATT_HEADS):
        h = hh // ATT_GROUP
        slope = 2.0 ** (-8.0 * (hh + 1) / ATT_HEADS)
        qh = q[:, hh * ATT_HEAD_DIM:(hh + 1) * ATT_HEAD_DIM]
        kh = k[:, h * ATT_HEAD_DIM:(h + 1) * ATT_HEAD_DIM]
        s = _nt_dot(qh, kh) - slope * distf
        s_scr[hh] = jnp.where(valid, s, -jnp.inf)
    for hh in range(ATT_HEADS):
        s = s_scr[hh]
        sink = sink_ref[hh]
        m = jnp.maximum(jnp.max(s, axis=-1, keepdims=True), sink)
        p = jnp.exp(s - m)
        r_scr[hh] = 1.0 / (jnp.sum(p, axis=-1, keepdims=True) + jnp.exp(sink - m))
        p_scr[hh] = p.astype(BF16)
    outs = []
    for hh in range(ATT_HEADS):
        h = hh // ATT_GROUP
        vh = v[:, h * ATT_HEAD_DIM:(h + 1) * ATT_HEAD_DIM]
        outs.append(jnp.dot(p_scr[hh], vh, preferred_element_type=F32) * r_scr[hh])
    o_ref[...] = jnp.concatenate(outs, axis=-1).astype(o_ref.dtype)


def _attention(proj, sink):
    B, S, _ = proj.shape
    blk = WINDOW
    nblk = S // blk
    kcol = COL_KA // ATT_KV
    vcol = COL_VA // ATT_KV

    def kv_spec(col, off):
        return pl.BlockSpec(
            (None, blk, ATT_KV),
            lambda b, j: (b, jnp.clip(j + off, 0, nblk - 1), col))

    return pl.pallas_call(
        functools.partial(_attn_kernel, nblk=nblk),
        grid=(B, nblk),
        in_specs=[
            pl.BlockSpec(memory_space=pltpu.SMEM),
            pl.BlockSpec((None, blk, ATT_Q), lambda b, j: (b, j, 0)),
            kv_spec(kcol, -1), kv_spec(kcol, 0), kv_spec(kcol, 1),
            kv_spec(vcol, -1), kv_spec(vcol, 0), kv_spec(vcol, 1),
        ],
        out_specs=pl.BlockSpec((None, blk, ATT_Q), lambda b, j: (b, j, 0)),
        out_shape=jax.ShapeDtypeStruct((B, S, ATT_Q), BF16),
        scratch_shapes=[pltpu.VMEM((ATT_HEADS, blk, 3 * blk), F32),
                        pltpu.VMEM((ATT_HEADS, blk, 3 * blk), BF16),
                        pltpu.VMEM((ATT_HEADS, blk, 1), F32)],
        compiler_params=_cparams(("parallel", "parallel")),
    )(sink, proj, proj, proj, proj, proj, proj, proj)


def _hgrn_prep(q_raw, z, lb, reverse):
    C = HG_CHUNK
    q = q_raw * jax.nn.sigmoid(q_raw)
    f = lb + (1.0 - lb) * jax.nn.sigmoid(z)
    kk = (1.0 - lb) * jax.nn.sigmoid(-z)
    logf = jnp.log(f) * (1.0 / math.log(2.0))
    row = lax.broadcasted_iota(jnp.int32, (C, C), 0)
    col = lax.broadcasted_iota(jnp.int32, (C, C), 1)
    tri = jnp.where((row <= col) if reverse else (row >= col), 1.0, 0.0).astype(F32)
    return q, kk, jnp.dot(tri, logf, preferred_element_type=F32, precision=HIGHEST)


def _hgrn_apply(q, kk, b, a, v, st_scr, reverse):
    C = HG_CHUNK
    edge = b[0:1, :] if reverse else b[C - 1:C, :]
    vb = v.astype(BF16)
    st = st_scr[...]
    o = _nt_dot((q * jnp.exp2(b)).astype(BF16), st.astype(BF16))
    kh = (kk * jnp.exp2(edge - b)).astype(BF16)
    st_scr[...] = st * jnp.exp2(edge) + _tn_dot(vb, kh)
    return o + jnp.dot(a, vb, preferred_element_type=F32)


def _hgrn_intra(q, kk, b, reverse):
    C = HG_CHUNK
    row = lax.broadcasted_iota(jnp.int32, (C, C), 0)
    col = lax.broadcasted_iota(jnp.int32, (C, C), 1)
    pos = lax.broadcasted_iota(jnp.int32, (C, 1), 0)
    shape3 = (C // SUBLANES, SUBLANES, HGRN_DK)
    b3 = b.reshape(shape3)
    sub = lax.broadcasted_iota(jnp.int32, shape3, 1)
    a = jnp.zeros((C, C), F32)
    for hsz in HG_LEVELS:
        seg = 2 * hsz
        if seg > SUBLANES:
            refs = []
            for s0 in range(0, C, seg):
                r = s0 + hsz if reverse else s0 + hsz - 1
                refs.append(jnp.broadcast_to(b[r:r + 1, :], (seg, HGRN_DK)))
            bmid = jnp.concatenate(refs, axis=0) if len(refs) > 1 else refs[0]
        else:
            bmid3 = None
            for s0 in reversed(range(0, SUBLANES, seg)):
                r = s0 + hsz if reverse else s0 + hsz - 1
                ref = jnp.broadcast_to(b3[:, r:r + 1, :], shape3)
                bmid3 = ref if bmid3 is None else jnp.where(sub < s0 + seg, ref, bmid3)
            bmid = bmid3.reshape(C, HGRN_DK)
        e = jnp.exp2(-jnp.abs(b - bmid))
        upper = (pos % seg) >= hsz
        qrole = jnp.logical_not(upper) if reverse else upper
        ql = jnp.where(qrole, q * e, 0.0).astype(BF16)
        kl = jnp.where(qrole, 0.0, kk * e).astype(BF16)
        al = _nt_dot(ql, kl)
        same = (row // seg) == (col // seg)
        rq = (row % seg) >= hsz
        ck = (col % seg) >= hsz
        if reverse:
            m = same & jnp.logical_not(rq) & ck
        else:
            m = same & rq & jnp.logical_not(ck)
        a = jnp.where(m, al, a)
    return jnp.where(row == col, jnp.sum(q * kk, axis=-1, keepdims=True), a).astype(BF16)


def _lower_bound(plb_ref):
    p = plb_ref[...]
    e = jnp.exp(p - jnp.max(p, axis=0, keepdims=True))
    return e[0:1, :] / jnp.sum(e, axis=0, keepdims=True)


def _hgrn_stage(q_ref, z_ref, lb, scr, nchunk, reverse):
    q_scr, k_scr, b_scr, a_scr = scr
    C = HG_CHUNK
    for ci in range(nchunk):
        sl = slice(ci * C, (ci + 1) * C)
        q_scr[sl, :], k_scr[sl, :], b_scr[sl, :] = _hgrn_prep(q_ref[sl, :], z_ref[sl, :], lb, reverse)
    for ci in range(nchunk):
        sl = slice(ci * C, (ci + 1) * C)
        a_scr[ci] = _hgrn_intra(q_scr[sl, :], k_scr[sl, :], b_scr[sl, :], reverse)


def _hgrn_fwd_kernel(q_ref, z_ref, v_ref, plb_ref, o_ref, st_scr, *scr, nchunk):
    @pl.when(pl.program_id(2) == 0)
    def _():
        st_scr[...] = jnp.zeros_like(st_scr)

    _hgrn_stage(q_ref, z_ref, _lower_bound(plb_ref), scr, nchunk, False)
    q_scr, k_scr, b_scr, a_scr = scr
    C = HG_CHUNK
    for ci in range(nchunk):
        sl = slice(ci * C, (ci + 1) * C)
        o_ref[sl, :] = _hgrn_apply(q_scr[sl, :], k_scr[sl, :], b_scr[sl, :], a_scr[ci], v_ref[sl, :], st_scr,
                                   False)


def _hgrn_bwd_kernel(q_ref, z_ref, v_ref, plb_ref, of_ref, gh_ref, gn_ref, o_ref, st_scr, *scr, nchunk):
    @pl.when(pl.program_id(2) == 0)
    def _():
        st_scr[...] = jnp.zeros_like(st_scr)

    _hgrn_stage(q_ref, z_ref, _lower_bound(plb_ref), scr, nchunk, True)
    q_scr, k_scr, b_scr, a_scr = scr
    C = HG_CHUNK
    for ci in reversed(range(nchunk)):
        sl = slice(ci * C, (ci + 1) * C)
        o = _hgrn_apply(q_scr[sl, :], k_scr[sl, :], b_scr[sl, :], a_scr[ci], v_ref[sl, :], st_scr, True)
        o = o + of_ref[sl, :]
        o = o * lax.rsqrt(jnp.mean(o * o, axis=-1, keepdims=True) + EPS) * gn_ref[...]
        gh = gh_ref[sl, :]
        o_ref[sl, :] = (o * (gh * jax.nn.sigmoid(gh))).astype(o_ref.dtype)


def _hgrn(proj, p_lb_fwd, p_lb_bwd, g_hnorm):
    B, S, _ = proj.shape
    tb = min(S, 1024)
    nt = S // tb
    nchunk = tb // HG_CHUNK
    hp = 1
    dk = HGRN_DK
    wd = dk
    scratch = ([pltpu.VMEM((dk, dk), F32)] + [pltpu.VMEM((tb, dk), F32) for _ in range(3)]
               + [pltpu.VMEM((nchunk, HG_CHUNK, HG_CHUNK), BF16)])

    def col_spec(col0, rev):
        if rev:
            return pl.BlockSpec((None, tb, wd), lambda b, h, t: (b, nt - 1 - t, col0 // wd + h))
        return pl.BlockSpec((None, tb, wd), lambda b, h, t: (b, t, col0 // wd + h))

    plb_spec = pl.BlockSpec((p_lb_fwd.shape[0], wd), lambda b, h, t: (0, h))
    sem = ("parallel", "parallel", "arbitrary")
    o_f = pl.pallas_call(
        functools.partial(_hgrn_fwd_kernel, nchunk=nchunk),
        grid=(B, HGRN_HEADS // hp, nt),
        in_specs=[col_spec(COL_QH, False), col_spec(COL_ZF, False), col_spec(COL_IH, False), plb_spec],
        out_specs=pl.BlockSpec((None, tb, wd), lambda b, h, t: (b, t, h)),
        out_shape=jax.ShapeDtypeStruct((B, S, HG_V), F32),
        scratch_shapes=scratch,
        compiler_params=_cparams(sem),
    )(proj, proj, proj, p_lb_fwd)
    return pl.pallas_call(
        functools.partial(_hgrn_bwd_kernel, nchunk=nchunk),
        grid=(B, HGRN_HEADS // hp, nt),
        in_specs=[col_spec(COL_QH, True), col_spec(COL_ZB, True), col_spec(COL_IH, True), plb_spec,
                  pl.BlockSpec((None, tb, wd), lambda b, h, t: (b, nt - 1 - t, h)),
                  col_spec(COL_GH, True),
                  pl.BlockSpec((1, wd), lambda b, h, t: (0, h))],
        out_specs=pl.BlockSpec((None, tb, wd), lambda b, h, t: (b, nt - 1 - t, h)),
        out_shape=jax.ShapeDtypeStruct((B, S, HG_V), BF16),
        scratch_shapes=scratch,
        compiler_params=_cparams(sem),
    )(proj, proj, proj, p_lb_bwd, o_f, proj, g_hnorm)


def _outproj_kernel(att_ref, hg_ref, ga0_ref, ga1_ref, gb0_ref, gb1_ref, x_ref, mod_ref, woa_ref, woh_ref,
                    wout_ref, g2_ref, wpq_ref, x1_ref, h2_ref, qp_ref):
    ya = jnp.dot(att_ref[...], woa_ref[...], preferred_element_type=F32)
    yh = jnp.dot(hg_ref[...], woh_ref[...], preferred_element_type=F32)
    ga = jnp.concatenate([ga0_ref[...], ga1_ref[...]], axis=-1)
    gb = jnp.concatenate([gb0_ref[...], gb1_ref[...]], axis=-1)
    merged = jax.nn.sigmoid(ga) * ya + jax.nn.sigmoid(gb) * yh
    x1 = x_ref[...] + mod_ref[2:3, :] * jnp.dot(merged.astype(BF16), wout_ref[...],
                                                preferred_element_type=F32)
    x1_ref[...] = x1
    y = x1 * lax.rsqrt(jnp.mean(x1 * x1, axis=-1, keepdims=True) + EPS) * g2_ref[...]
    h2 = y * (1.0 + mod_ref[4:5, :]) + mod_ref[3:4, :]
    h2_ref[...] = h2
    qp_ref[...] = jnp.dot(h2.astype(BF16), wpq_ref[...], preferred_element_type=F32)


def _outproj(att, hg, proj, x, mod3, woa, woh, wout, g2, wpq):
    B, S, _ = x.shape
    tm = 256
    half = D_MODEL // 2
    nq = wpq.shape[1]

    def tok(width, col):
        return pl.BlockSpec((None, tm, width), lambda b, i: (b, i, col))

    def full(shape):
        return pl.BlockSpec(shape, lambda b, i: (0, 0))

    return pl.pallas_call(
        _outproj_kernel,
        grid=(B, S // tm),
        in_specs=[
            tok(ATT_Q, 0), tok(HG_V, 0),
            tok(half, COL_GA // half), tok(half, COL_GA // half + 1),
            tok(half, COL_GB // half), tok(half, COL_GB // half + 1),
            tok(D_MODEL, 0),
            pl.BlockSpec((None, 6, D_MODEL), lambda b, i: (b, 0, 0)),
            full((ATT_Q, D_MODEL)), full((HG_V, D_MODEL)), full((D_MODEL, D_MODEL)),
            full((1, D_MODEL)), full((D_MODEL, nq)),
        ],
        out_specs=[tok(D_MODEL, 0), tok(D_MODEL, 0), tok(nq, 0)],
        out_shape=[jax.ShapeDtypeStruct((B, S, D_MODEL), F32),
                   jax.ShapeDtypeStruct((B, S, D_MODEL), F32),
                   jax.ShapeDtypeStruct((B, S, nq), F32)],
        compiler_params=_cparams(("parallel", "parallel")),
    )(att, hg, proj, proj, proj, proj, x, mod3, woa, woh, wout, g2, wpq)


ROUTE_HEADS = 4
ROUTE_TT = 256


def _route_kernel(q_ref, keys_ref, eid_ref, gate_ref, sv_scr, si_scr, cand_scr, cid_scr, ts_scr):
    tt = q_ref.shape[0]
    n = PEER_NKEYS
    k = PEER_TOPK
    nch = 2 * ROUTE_HEADS
    q = q_ref[...].astype(BF16)
    iota_n = lax.broadcasted_iota(jnp.int32, (n, tt), 0).astype(F32)
    s = [_nt_dot(keys_ref[c * n:(c + 1) * n, :].astype(BF16), q[:, c * PEER_HALF:(c + 1) * PEER_HALF])
         for c in range(nch)]
    for i in range(k):
        for c in range(nch):
            m = jnp.max(s[c], axis=0, keepdims=True)
            tagged = jnp.where(s[c] == m, iota_n, float(n))
            idx = jnp.min(tagged, axis=0, keepdims=True)
            sv_scr[c, i:i + 1, :] = m
            si_scr[c, i:i + 1, :] = idx
            s[c] = jnp.where(tagged == idx, -jnp.inf, s[c])
    ncand = cand_scr.shape[1]
    cand_scr[...] = jnp.full(cand_scr.shape, -jnp.inf, F32)
    cid_scr[...] = jnp.zeros(cid_scr.shape, F32)
    for h in range(ROUTE_HEADS):
        off = 0
        for a in range(k):
            nb = k // (a + 1)
            cand_scr[h, off:off + nb, :] = sv_scr[2 * h, a:a + 1, :] + sv_scr[2 * h + 1, 0:nb, :]
            cid_scr[h, off:off + nb, :] = si_scr[2 * h, a:a + 1, :] * float(n) + si_scr[2 * h + 1, 0:nb, :]
            off += nb
    cs = [cand_scr[h] for h in range(ROUTE_HEADS)]
    ids = [cid_scr[h] for h in range(ROUTE_HEADS)]
    iota_c = lax.broadcasted_iota(jnp.int32, (ncand, tt), 0).astype(F32)
    for i in range(k):
        for h in range(ROUTE_HEADS):
            m = jnp.max(cs[h], axis=0, keepdims=True)
            tagged = jnp.where(cs[h] == m, iota_c, float(ncand))
            sel = tagged == jnp.min(tagged, axis=0, keepdims=True)
            r = h * k + i
            eid_ref[r:r + 1, :] = jnp.max(jnp.where(sel, ids[h], -1.0), axis=0, keepdims=True).astype(jnp.int32)
            ts_scr[r:r + 1, :] = m
            cs[h] = jnp.where(sel, -jnp.inf, cs[h])
    for h in range(ROUTE_HEADS):
        ts = ts_scr[h * k:(h + 1) * k, :]
        ex = jnp.exp(ts - ts[0:1, :])
        gate_ref[h * k:(h + 1) * k, :] = ex / jnp.sum(ex, axis=0, keepdims=True)


def _route(qp, keys2d):
    n_tok = qp.shape[0]
    tt = ROUTE_TT
    hp = ROUTE_HEADS
    k = PEER_TOPK
    npairs = sum(k // (a + 1) for a in range(k))
    ncand = -(-npairs // SUBLANES) * SUBLANES
    return pl.pallas_call(
        _route_kernel,
        grid=(n_tok // tt, PEER_HEADS // hp),
        in_specs=[
            pl.BlockSpec((tt, hp * 2 * PEER_HALF), lambda i, h: (i, h)),
            pl.BlockSpec((hp * 2 * PEER_NKEYS, PEER_HALF), lambda i, h: (h, 0)),
        ],
        out_specs=[pl.BlockSpec((hp * k, tt), lambda i, h: (h, i)),
                   pl.BlockSpec((hp * k, tt), lambda i, h: (h, i))],
        out_shape=[jax.ShapeDtypeStruct((PEER_SEL, n_tok), jnp.int32),
                   jax.ShapeDtypeStruct((PEER_SEL, n_tok), F32)],
        scratch_shapes=[pltpu.VMEM((2 * hp, k, tt), F32), pltpu.VMEM((2 * hp, k, tt), F32),
                        pltpu.VMEM((hp, ncand, tt), F32), pltpu.VMEM((hp, ncand, tt), F32),
                        pltpu.VMEM((hp * k, tt), F32)],
        compiler_params=_cparams(("parallel", "arbitrary")),
    )(qp, keys2d)


PEER_GROUP = SUBLANES
PEER_NGROUP = 4
PEER_TT = PEER_GROUP * PEER_NGROUP
U_MASK = 0xFFFF0000


PEER_CHUNKS = D_MODEL // LANES
PEER_SEL_TILES = PEER_SEL // SUBLANES


def _pack_tables(u, v):
    ub = lax.bitcast_convert_type(u.astype(BF16), jnp.uint16).astype(jnp.uint32)
    vb = lax.bitcast_convert_type(v.astype(BF16), jnp.uint16).astype(jnp.uint32)
    return ((ub << 16) | vb).reshape(u.shape[0], PEER_CHUNKS, LANES)


def _peer_kernel(eid_ref, eidn_ref, gate_ref, h2_ref, x1_ref, mod_ref, gf_ref, tab_hbm, o_ref, wbuf, sem, w_scr):
    i = pl.program_id(0)
    n = pl.num_programs(0)
    slot = i % 2
    G = PEER_GROUP

    def issue_rows(idx_ref, g, s, tok, j0, j1):
        for j in range(j0, j1):
            e = idx_ref[tok, j]
            pltpu.make_async_copy(tab_hbm.at[e], wbuf.at[s, tok, j], sem.at[s, g]).start(priority=j % 2)

    def wait_group(g, s):
        rows = pl.ds(g * G, G)
        pltpu.make_async_copy(wbuf.at[s, rows], wbuf.at[s, rows], sem.at[s, g]).wait()

    per_step = PEER_SEL // (2 * PEER_SEL_TILES)
    sub8 = lax.broadcasted_iota(jnp.int32, (SUBLANES, LANES), 0)

    def fold(x, y, h):
        low = (sub8 % (2 * h)) < h
        if 2 * h == SUBLANES:
            return jnp.where(low, x, y) + pltpu.roll(jnp.where(low, y, x), h, 0)
        return jnp.where(low, x + pltpu.roll(x, SUBLANES - h, 0), y + pltpu.roll(y, h, 0))

    fold_order = (0, 4, 2, 6, 1, 5, 3, 7)

    def chunk_sums(tiles):
        t = [tiles[k] for k in fold_order]
        t = [fold(t[0], t[1], 4), fold(t[2], t[3], 4), fold(t[4], t[5], 4), fold(t[6], t[7], 4)]
        t = [fold(t[0], t[1], 2), fold(t[2], t[3], 2)]
        return fold(t[0], t[1], 1)

    def compute_group(g, s):
        rows = pl.ds(pl.multiple_of(g * G, G), G)
        gate_t = gate_ref[rows, :].T
        h2g = h2_ref[rows, :]
        sub = lax.broadcasted_iota(jnp.int32, (G, D_MODEL), 0)
        y = jnp.zeros((G, D_MODEL), F32)
        for r in range(G):
            tok = g * G + r
            h2t = jnp.zeros((SUBLANES, LANES), F32)
            for c in range(PEER_CHUNKS):
                h2t = jnp.where(sub8 == c, h2g[r:r + 1, c * LANES:(c + 1) * LANES], h2t)
            part = []
            for a in range(PEER_SEL_TILES):
                issue_rows(eidn_ref, g, 1 - s, tok, a * per_step, (a + 1) * per_step)
                part.append(chunk_sums([
                    lax.bitcast_convert_type(wbuf[s, tok, a * SUBLANES + b] & jnp.uint32(U_MASK), F32) * h2t
                    for b in range(SUBLANES)]))
            act = jnp.sum(jnp.stack(part), axis=-1, keepdims=True)
            act = 0.5 * act * (1.0 + lax.erf(act * (1.0 / math.sqrt(2.0))))
            w = gate_t[:, r:r + 1].reshape(PEER_SEL_TILES, SUBLANES, 1) * act
            w_scr[...] = jnp.broadcast_to(w, (PEER_SEL_TILES, SUBLANES, LANES))
            acc = jnp.zeros((SUBLANES, LANES), F32)
            for a in range(PEER_SEL_TILES):
                issue_rows(eidn_ref, g, 1 - s, tok, (PEER_SEL_TILES + a) * per_step,
                           (PEER_SEL_TILES + a + 1) * per_step)
                for b in range(SUBLANES):
                    vf = lax.bitcast_convert_type(wbuf[s, tok, a * SUBLANES + b] << 16, F32)
                    acc = acc + w_scr[a, b:b + 1, :] * vf
            yrow = jnp.concatenate([acc[c:c + 1, :] for c in range(PEER_CHUNKS)], axis=-1)
            y = jnp.where(sub == r, yrow, y)
        x2 = x1_ref[rows, :] + mod_ref[5:6, :] * y
        o_ref[rows, :] = x2 * lax.rsqrt(jnp.mean(x2 * x2, axis=-1, keepdims=True) + EPS) * gf_ref[...]

    @pl.when(i == 0)
    def _():
        def first(t, carry):
            issue_rows(eid_ref, t // G, 0, t, 0, PEER_SEL)
            return carry
        lax.fori_loop(0, PEER_TT, first, 0)

    def step(g, carry):
        wait_group(g, slot)
        compute_group(g, slot)
        return carry

    lax.fori_loop(0, PEER_NGROUP, step, 0)

    @pl.when(i == n - 1)
    def _():
        def drain(g, carry):
            wait_group(g, 1 - slot)
            return carry
        lax.fori_loop(0, PEER_NGROUP, drain, 0)


def _peer(eid_t, gate_t, h2, x1, mod3, g_final, table, seq):
    n_tok = h2.shape[0]
    tt = PEER_TT
    n = n_tok // tt
    smem_idx = lambda f: pl.BlockSpec((tt, PEER_SEL), f, memory_space=pltpu.SMEM)
    tok = lambda w: pl.BlockSpec((tt, w), lambda i: (i, 0))
    return pl.pallas_call(
        _peer_kernel,
        grid=(n,),
        in_specs=[
            smem_idx(lambda i: (i, 0)),
            smem_idx(lambda i: (jnp.minimum(i + 1, n - 1), 0)),
            tok(PEER_SEL), tok(D_MODEL), tok(D_MODEL),
            pl.BlockSpec((None, 6, D_MODEL), lambda i: ((i * tt) // seq, 0, 0)),
            pl.BlockSpec((1, D_MODEL), lambda i: (0, 0)),
            pl.BlockSpec(memory_space=pl.ANY),
        ],
        out_specs=tok(D_MODEL),
        out_shape=jax.ShapeDtypeStruct((n_tok, D_MODEL), F32),
        scratch_shapes=[pltpu.VMEM((2, tt, PEER_SEL, PEER_CHUNKS, LANES), jnp.uint32),
                        pltpu.SemaphoreType.DMA((2, PEER_NGROUP)),
                        pltpu.VMEM((PEER_SEL_TILES, SUBLANES, LANES), F32)],
        compiler_params=_cparams(("arbitrary",)),
    )(eid_t, eid_t, gate_t, h2, x1, mod3, g_final, table)


def _run_group(x, c, w_mod, b_mod, g1, w_in_bf, sink, p_lb_fwd, p_lb_bwd, g_hnorm, woa, woh, wout, g2,
               wpq, keys2d, table, g_final):
    B, S, _ = x.shape
    mod3 = _modulation(c, w_mod, b_mod).reshape(B, 6, D_MODEL)
    proj = _inproj(x, mod3, g1, w_in_bf)
    att = _attention(proj, sink)
    hg = _hgrn(proj, p_lb_fwd, p_lb_bwd, g_hnorm)
    x1, h2, qp = _outproj(att, hg, proj, x, mod3, woa, woh, wout, g2, wpq)
    n_tok = B * S
    eid, gate = _route(qp.reshape(n_tok, -1), keys2d)
    y = _peer(eid.T, gate.T, h2.reshape(n_tok, D_MODEL), x1.reshape(n_tok, D_MODEL), mod3, g_final, table, S)
    return y.reshape(B, S, D_MODEL)


def kernel(x_prompt, x_sample, c_prompt, c_sample, w_mod, b_mod, g_norm1, w_in, att_sink, p_lb_fwd, p_lb_bwd,
           g_hnorm, w_o_att, w_o_hgrn, w_out, g_norm2, w_pq, peer_keys, peer_u, peer_v, g_final):
    assert w_mod.shape[0] == 1 and p_lb_fwd.shape[0] == 2, "single-layer encoder"
    shared = (
        w_mod[0], b_mod[0], g_norm1[0].reshape(1, D_MODEL), w_in[0].astype(BF16), att_sink[0],
        p_lb_fwd, p_lb_bwd, g_hnorm[0].reshape(1, HG_V),
        w_o_att[0].astype(BF16), w_o_hgrn[0].astype(BF16), w_out[0].astype(BF16),
        g_norm2[0].reshape(1, D_MODEL), w_pq[0].astype(BF16),
        peer_keys[0].reshape(PEER_HEADS * 2 * PEER_NKEYS, PEER_HALF),
        _pack_tables(peer_u[0], peer_v[0]), g_final.reshape(1, D_MODEL),
    )
    return (_run_group(x_prompt, c_prompt, *shared), _run_group(x_sample, c_sample, *shared))
```

```python
import functools
import math

import jax
import jax.numpy as jnp
from jax import lax
from jax.experimental import pallas as pl
from jax.experimental.pallas import tpu as pltpu

F32 = jnp.float32
BF16 = jnp.bfloat16
HIGHEST = lax.Precision.HIGHEST

D_MODEL = 1024
ATT_HEADS = 16
ATT_KV_HEADS = 4
ATT_HEAD_DIM = 64
ATT_GROUP = ATT_HEADS // ATT_KV_HEADS
WINDOW = 128
ATT_Q = ATT_HEADS * ATT_HEAD_DIM
ATT_KV = ATT_KV_HEADS * ATT_HEAD_DIM
HGRN_HEADS = 8
HGRN_DK = 128
HG_K = HGRN_HEADS * HGRN_DK
HG_V = HG_K
IN_COLS = ATT_Q + 2 * ATT_KV + 3 * HG_K + 2 * HG_V + 2 * D_MODEL
PEER_HEADS = 8
PEER_NKEYS = 128
PEER_HALF = 128
PEER_TOPK = 16
PEER_SEL = PEER_HEADS * PEER_TOPK
EPS = 1e-6

COL_QA = 0
COL_KA = COL_QA + ATT_Q
COL_VA = COL_KA + ATT_KV
COL_QH = COL_VA + ATT_KV
COL_ZF = COL_QH + HG_K
COL_ZB = COL_ZF + HG_K
COL_IH = COL_ZB + HG_K
COL_GH = COL_IH + HG_V
COL_GA = COL_GH + HG_V
COL_GB = COL_GA + D_MODEL

LANES = 128
SUBLANES = 8
VMEM_LIMIT = 56 * 1024 * 1024

HG_CHUNK = 128
HG_LEVELS = (64, 32, 16, 8, 4, 2, 1)


def _cparams(sem):
    return pltpu.CompilerParams(dimension_semantics=sem, vmem_limit_bytes=VMEM_LIMIT)


def _nt_dot(a, b):
    return lax.dot_general(a, b, (((1,), (1,)), ((), ())), preferred_element_type=F32)


def _tn_dot(a, b):
    return lax.dot_general(a, b, (((0,), (0,)), ((), ())), preferred_element_type=F32)


def _mod_kernel(c_ref, w_ref, b_ref, o_ref):
    c = c_ref[...]
    sc = c * jax.nn.sigmoid(c)
    o_ref[...] = jnp.dot(sc, w_ref[...], preferred_element_type=F32, precision=HIGHEST) + b_ref[...]


def _modulation(c, w_mod, b_mod):
    B = c.shape[0]
    n = w_mod.shape[1]
    tn = D_MODEL
    return pl.pallas_call(
        _mod_kernel,
        grid=(n // tn,),
        in_specs=[
            pl.BlockSpec((B, D_MODEL), lambda j: (0, 0)),
            pl.BlockSpec((D_MODEL, tn), lambda j: (0, j)),
            pl.BlockSpec((1, tn), lambda j: (0, j)),
        ],
        out_specs=pl.BlockSpec((B, tn), lambda j: (0, j)),
        out_shape=jax.ShapeDtypeStruct((B, n), F32),
        compiler_params=_cparams(("parallel",)),
    )(c, w_mod, b_mod.reshape(1, n))


def _inproj_kernel(x_ref, mod_ref, g_ref, w_ref, o_ref, h_scr):
    @pl.when(pl.program_id(2) == 0)
    def _():
        x = x_ref[...]
        y = x * lax.rsqrt(jnp.mean(x * x, axis=-1, keepdims=True) + EPS) * g_ref[...]
        h = y * (1.0 + mod_ref[1:2, :]) + mod_ref[0:1, :]
        h_scr[...] = h.astype(BF16)

    o_ref[...] = jnp.dot(h_scr[...], w_ref[...], preferred_element_type=F32)


def _inproj(x, mod3, g1, w_in_bf):
    B, S, _ = x.shape
    tm = min(S, 1024)
    tn = 2176
    return pl.pallas_call(
        _inproj_kernel,
        grid=(B, S // tm, IN_COLS // tn),
        in_specs=[
            pl.BlockSpec((None, tm, D_MODEL), lambda b, i, j: (b, i, 0)),
            pl.BlockSpec((None, 6, D_MODEL), lambda b, i, j: (b, 0, 0)),
            pl.BlockSpec((1, D_MODEL), lambda b, i, j: (0, 0)),
            pl.BlockSpec((D_MODEL, tn), lambda b, i, j: (0, j)),
        ],
        out_specs=pl.BlockSpec((None, tm, tn), lambda b, i, j: (b, i, j)),
        out_shape=jax.ShapeDtypeStruct((B, S, IN_COLS), F32),
        scratch_shapes=[pltpu.VMEM((tm, D_MODEL), BF16)],
        compiler_params=_cparams(("parallel", "parallel", "arbitrary")),
    )(x, mod3, g1, w_in_bf)


def _attn_kernel(sink_ref, q_ref, kp_ref, kc_ref, kn_ref, vp_ref, vc_ref, vn_ref, o_ref, s_scr, p_scr, r_scr,
                 *, nblk):
    j = pl.program_id(1)
    blk = WINDOW
    span = 3 * blk
    scale = 1.0 / math.sqrt(ATT_HEAD_DIM)
    q = (q_ref[...] * scale).astype(BF16)
    k = jnp.concatenate([kp_ref[...], kc_ref[...], kn_ref[...]], axis=0).astype(BF16)
    v = jnp.concatenate([vp_ref[...], vc_ref[...], vn_ref[...]], axis=0).astype(BF16)
    qi = lax.broadcasted_iota(jnp.int32, (blk, span), 0)
    kj = lax.broadcasted_iota(jnp.int32, (blk, span), 1)
    dist = jnp.abs(kj - blk - qi)
    lo = jnp.where(j == 0, blk, 0)
    hi = jnp.where(j == nblk - 1, 2 * blk, span)
    valid = (dist <= WINDOW) & (kj >= lo) & (kj < hi)
    neg_dist = jnp.where(valid, -dist.astype(F32), -jnp.inf)
    for hh in range(ATT_HEADS):
        h = hh // ATT_GROUP
        slope = 2.0 ** (-8.0 * (hh + 1) / ATT_HEADS)
        qh = q[:, hh * ATT_HEAD_DIM:(hh + 1) * ATT_HEAD_DIM]
        kh = k[:, h * ATT_HEAD_DIM:(h + 1) * ATT_HEAD_DIM]
        s_scr[hh] = _nt_dot(qh, kh) + slope * neg_dist
    for hh in range(ATT_HEADS):
        s = s_scr[hh]
        sink = sink_ref[hh]
        m = jnp.maximum(jnp.max(s, axis=-1, keepdims=True), sink)
        p = jnp.exp(s - m)
        r_scr[hh] = 1.0 / (jnp.sum(p, axis=-1, keepdims=True) + jnp.exp(sink - m))
        p_scr[hh] = p.astype(BF16)
    outs = []
    for hh in range(ATT_HEADS):
        h = hh // ATT_GROUP
        vh = v[:, h * ATT_HEAD_DIM:(h + 1) * ATT_HEAD_DIM]
        outs.append(jnp.dot(p_scr[hh], vh, preferred_element_type=F32) * r_scr[hh])
    o_ref[...] = jnp.concatenate(outs, axis=-1).astype(o_ref.dtype)


def _attention(proj, sink):
    B, S, _ = proj.shape
    blk = WINDOW
    nblk = S // blk
    kcol = COL_KA // ATT_KV
    vcol = COL_VA // ATT_KV

    def kv_spec(col, off):
        return pl.BlockSpec(
            (None, blk, ATT_KV),
            lambda b, j: (b, jnp.clip(j + off, 0, nblk - 1), col))

    return pl.pallas_call(
        functools.partial(_attn_kernel, nblk=nblk),
        grid=(B, nblk),
        in_specs=[
            pl.BlockSpec(memory_space=pltpu.SMEM),
            pl.BlockSpec((None, blk, ATT_Q), lambda b, j: (b, j, 0)),
            kv_spec(kcol, -1), kv_spec(kcol, 0), kv_spec(kcol, 1),
            kv_spec(vcol, -1), kv_spec(vcol, 0), kv_spec(vcol, 1),
        ],
        out_specs=pl.BlockSpec((None, blk, ATT_Q), lambda b, j: (b, j, 0)),
        out_shape=jax.ShapeDtypeStruct((B, S, ATT_Q), BF16),
        scratch_shapes=[pltpu.VMEM((ATT_HEADS, blk, 3 * blk), F32),
                        pltpu.VMEM((ATT_HEADS, blk, 3 * blk), BF16),
                        pltpu.VMEM((ATT_HEADS, blk, 1), F32)],
        compiler_params=_cparams(("parallel", "parallel")),
    )(sink, proj, proj, proj, proj, proj, proj, proj)


def _hgrn_prep(q_raw, z, lb, reverse):
    C = HG_CHUNK
    q = q_raw * jax.nn.sigmoid(q_raw)
    f = lb + (1.0 - lb) * jax.nn.sigmoid(z)
    kk = (1.0 - lb) * jax.nn.sigmoid(-z)
    logf = jnp.log(f) * (1.0 / math.log(2.0))
    row = lax.broadcasted_iota(jnp.int32, (C, C), 0)
    col = lax.broadcasted_iota(jnp.int32, (C, C), 1)
    tri = jnp.where((row <= col) if reverse else (row >= col), 1.0, 0.0).astype(F32)
    return q, kk, jnp.dot(tri, logf, preferred_element_type=F32, precision=HIGHEST)


def _hgrn_apply(q, kk, b, a, v, st_scr, reverse):
    C = HG_CHUNK
    edge = b[0:1, :] if reverse else b[C - 1:C, :]
    vb = v.astype(BF16)
    st = st_scr[...]
    o = _nt_dot((q * jnp.exp2(b)).astype(BF16), st.astype(BF16))
    kh = (kk * jnp.exp2(edge - b)).astype(BF16)
    st_scr[...] = st * jnp.exp2(edge) + _tn_dot(vb, kh)
    return o + jnp.dot(a, vb, preferred_element_type=F32)


def _hgrn_intra(q, kk, b, reverse):
    C = HG_CHUNK
    row = lax.broadcasted_iota(jnp.int32, (C, C), 0)
    col = lax.broadcasted_iota(jnp.int32, (C, C), 1)
    pos = lax.broadcasted_iota(jnp.int32, (C, 1), 0)
    shape3 = (C // SUBLANES, SUBLANES, HGRN_DK)
    b3 = b.reshape(shape3)
    sub = lax.broadcasted_iota(jnp.int32, shape3, 1)
    a = jnp.zeros((C, C), F32)
    for hsz in HG_LEVELS:
        seg = 2 * hsz
        if seg > SUBLANES:
            refs = []
            for s0 in range(0, C, seg):
                r = s0 + hsz if reverse else s0 + hsz - 1
                refs.append(jnp.broadcast_to(b[r:r + 1, :], (seg, HGRN_DK)))
            bmid = jnp.concatenate(refs, axis=0) if len(refs) > 1 else refs[0]
        else:
            bmid3 = None
            for s0 in reversed(range(0, SUBLANES, seg)):
                r = s0 + hsz if reverse else s0 + hsz - 1
                ref = jnp.broadcast_to(b3[:, r:r + 1, :], shape3)
                bmid3 = ref if bmid3 is None else jnp.where(sub < s0 + seg, ref, bmid3)
            bmid = bmid3.reshape(C, HGRN_DK)
        e = jnp.exp2(-jnp.abs(b - bmid))
        upper = (pos % seg) >= hsz
        qrole = jnp.logical_not(upper) if reverse else upper
        ql = jnp.where(qrole, q * e, 0.0).astype(BF16)
        kl = jnp.where(qrole, 0.0, kk * e).astype(BF16)
        al = _nt_dot(ql, kl)
        same = (row // seg) == (col // seg)
        rq = (row % seg) >= hsz
        ck = (col % seg) >= hsz
        if reverse:
            m = same & jnp.logical_not(rq) & ck
        else:
            m = same & rq & jnp.logical_not(ck)
        a = jnp.where(m, al, a)
    return jnp.where(row == col, jnp.sum(q * kk, axis=-1, keepdims=True), a).astype(BF16)


def _lower_bound(plb_ref):
    p = plb_ref[...]
    e = jnp.exp(p - jnp.max(p, axis=0, keepdims=True))
    return e[0:1, :] / jnp.sum(e, axis=0, keepdims=True)


def _hgrn_stage(q_ref, z_ref, lb, scr, nchunk, reverse):
    q_scr, k_scr, b_scr, a_scr = scr
    C = HG_CHUNK
    for ci in range(nchunk):
        sl = slice(ci * C, (ci + 1) * C)
        q_scr[sl, :], k_scr[sl, :], b_scr[sl, :] = _hgrn_prep(q_ref[sl, :], z_ref[sl, :], lb, reverse)
    for ci in range(nchunk):
        sl = slice(ci * C, (ci + 1) * C)
        a_scr[ci] = _hgrn_intra(q_scr[sl, :], k_scr[sl, :], b_scr[sl, :], reverse)


def _hgrn_fwd_kernel(q_ref, z_ref, v_ref, plb_ref, o_ref, st_scr, *scr, nchunk):
    @pl.when(pl.program_id(2) == 0)
    def _():
        st_scr[...] = jnp.zeros_like(st_scr)

    _hgrn_stage(q_ref, z_ref, _lower_bound(plb_ref), scr, nchunk, False)
    q_scr, k_scr, b_scr, a_scr = scr
    C = HG_CHUNK
    for ci in range(nchunk):
        sl = slice(ci * C, (ci + 1) * C)
        o_ref[sl, :] = _hgrn_apply(q_scr[sl, :], k_scr[sl, :], b_scr[sl, :], a_scr[ci], v_ref[sl, :], st_scr,
                                   False)


def _hgrn_bwd_kernel(q_ref, z_ref, v_ref, plb_ref, of_ref, gh_ref, gn_ref, o_ref, st_scr, *scr, nchunk):
    @pl.when(pl.program_id(2) == 0)
    def _():
        st_scr[...] = jnp.zeros_like(st_scr)

    _hgrn_stage(q_ref, z_ref, _lower_bound(plb_ref), scr, nchunk, True)
    q_scr, k_scr, b_scr, a_scr = scr
    C = HG_CHUNK
    for ci in reversed(range(nchunk)):
        sl = slice(ci * C, (ci + 1) * C)
        o = _hgrn_apply(q_scr[sl, :], k_scr[sl, :], b_scr[sl, :], a_scr[ci], v_ref[sl, :], st_scr, True)
        o = o + of_ref[sl, :]
        o = o * lax.rsqrt(jnp.mean(o * o, axis=-1, keepdims=True) + EPS) * gn_ref[...]
        gh = gh_ref[sl, :]
        o_ref[sl, :] = (o * (gh * jax.nn.sigmoid(gh))).astype(o_ref.dtype)


def _hgrn(proj, p_lb_fwd, p_lb_bwd, g_hnorm):
    B, S, _ = proj.shape
    tb = min(S, 2048)
    nt = S // tb
    nchunk = tb // HG_CHUNK
    hp = 1
    dk = HGRN_DK
    wd = dk
    scratch = ([pltpu.VMEM((dk, dk), F32)] + [pltpu.VMEM((tb, dk), F32) for _ in range(3)]
               + [pltpu.VMEM((nchunk, HG_CHUNK, HG_CHUNK), BF16)])

    def col_spec(col0, rev):
        if rev:
            return pl.BlockSpec((None, tb, wd), lambda b, h, t: (b, nt - 1 - t, col0 // wd + h))
        return pl.BlockSpec((None, tb, wd), lambda b, h, t: (b, t, col0 // wd + h))

    plb_spec = pl.BlockSpec((p_lb_fwd.shape[0], wd), lambda b, h, t: (0, h))
    sem = ("parallel", "parallel", "arbitrary")
    o_f = pl.pallas_call(
        functools.partial(_hgrn_fwd_kernel, nchunk=nchunk),
        grid=(B, HGRN_HEADS // hp, nt),
        in_specs=[col_spec(COL_QH, False), col_spec(COL_ZF, False), col_spec(COL_IH, False), plb_spec],
        out_specs=pl.BlockSpec((None, tb, wd), lambda b, h, t: (b, t, h)),
        out_shape=jax.ShapeDtypeStruct((B, S, HG_V), F32),
        scratch_shapes=scratch,
        compiler_params=_cparams(sem),
    )(proj, proj, proj, p_lb_fwd)
    return pl.pallas_call(
        functools.partial(_hgrn_bwd_kernel, nchunk=nchunk),
        grid=(B, HGRN_HEADS // hp, nt),
        in_specs=[col_spec(COL_QH, True), col_spec(COL_ZB, True), col_spec(COL_IH, True), plb_spec,
                  pl.BlockSpec((None, tb, wd), lambda b, h, t: (b, nt - 1 - t, h)),
                  col_spec(COL_GH, True),
                  pl.BlockSpec((1, wd), lambda b, h, t: (0, h))],
        out_specs=pl.BlockSpec((None, tb, wd), lambda b, h, t: (b, nt - 1 - t, h)),
        out_shape=jax.ShapeDtypeStruct((B, S, HG_V), BF16),
        scratch_shapes=scratch,
        compiler_params=_cparams(sem),
    )(proj, proj, proj, p_lb_bwd, o_f, proj, g_hnorm)


def _outproj_kernel(att_ref, hg_ref, ga0_ref, ga1_ref, gb0_ref, gb1_ref, x_ref, mod_ref, woa_ref, woh_ref,
                    wout_ref, g2_ref, wpq_ref, x1_ref, h2_ref, qp_ref):
    ya = jnp.dot(att_ref[...], woa_ref[...], preferred_element_type=F32)
    yh = jnp.dot(hg_ref[...], woh_ref[...], preferred_element_type=F32)
    ga = jnp.concatenate([ga0_ref[...], ga1_ref[...]], axis=-1)
    gb = jnp.concatenate([gb0_ref[...], gb1_ref[...]], axis=-1)
    merged = jax.nn.sigmoid(ga) * ya + jax.nn.sigmoid(gb) * yh
    x1 = x_ref[...] + mod_ref[2:3, :] * jnp.dot(merged.astype(BF16), wout_ref[...],
                                                preferred_element_type=F32)
    x1_ref[...] = x1
    y = x1 * lax.rsqrt(jnp.mean(x1 * x1, axis=-1, keepdims=True) + EPS) * g2_ref[...]
    h2 = y * (1.0 + mod_ref[4:5, :]) + mod_ref[3:4, :]
    h2_ref[...] = h2
    qp_ref[...] = jnp.dot(h2.astype(BF16), wpq_ref[...], preferred_element_type=F32)


def _outproj(att, hg, proj, x, mod3, woa, woh, wout, g2, wpq):
    B, S, _ = x.shape
    tm = 256
    half = D_MODEL // 2
    nq = wpq.shape[1]

    def tok(width, col):
        return pl.BlockSpec((None, tm, width), lambda b, i: (b, i, col))

    def full(shape):
        return pl.BlockSpec(shape, lambda b, i: (0, 0))

    return pl.pallas_call(
        _outproj_kernel,
        grid=(B, S // tm),
        in_specs=[
            tok(ATT_Q, 0), tok(HG_V, 0),
            tok(half, COL_GA // half), tok(half, COL_GA // half + 1),
            tok(half, COL_GB // half), tok(half, COL_GB // half + 1),
            tok(D_MODEL, 0),
            pl.BlockSpec((None, 6, D_MODEL), lambda b, i: (b, 0, 0)),
            full((ATT_Q, D_MODEL)), full((HG_V, D_MODEL)), full((D_MODEL, D_MODEL)),
            full((1, D_MODEL)), full((D_MODEL, nq)),
        ],
        out_specs=[tok(D_MODEL, 0), tok(D_MODEL, 0), tok(nq, 0)],
        out_shape=[jax.ShapeDtypeStruct((B, S, D_MODEL), F32),
                   jax.ShapeDtypeStruct((B, S, D_MODEL), F32),
                   jax.ShapeDtypeStruct((B, S, nq), F32)],
        compiler_params=_cparams(("parallel", "parallel")),
    )(att, hg, proj, proj, proj, proj, x, mod3, woa, woh, wout, g2, wpq)


ROUTE_HEADS = 4
ROUTE_TT = 256


def _route_kernel(q_ref, keys_ref, eid_ref, gate_ref, sv_scr, si_scr, cand_scr, cid_scr, ts_scr):
    tt = q_ref.shape[0]
    n = PEER_NKEYS
    k = PEER_TOPK
    nch = 2 * ROUTE_HEADS
    q = q_ref[...].astype(BF16)
    iota_n = lax.broadcasted_iota(jnp.int32, (n, tt), 0).astype(F32)
    s = [_nt_dot(keys_ref[c * n:(c + 1) * n, :].astype(BF16), q[:, c * PEER_HALF:(c + 1) * PEER_HALF])
         for c in range(nch)]
    for i in range(k):
        for c in range(nch):
            m = jnp.max(s[c], axis=0, keepdims=True)
            tagged = jnp.where(s[c] == m, iota_n, float(n))
            idx = jnp.min(tagged, axis=0, keepdims=True)
            sv_scr[c, i:i + 1, :] = m
            si_scr[c, i:i + 1, :] = idx
            s[c] = jnp.where(tagged == idx, -jnp.inf, s[c])
    ncand = cand_scr.shape[1]
    cand_scr[...] = jnp.full(cand_scr.shape, -jnp.inf, F32)
    cid_scr[...] = jnp.zeros(cid_scr.shape, F32)
    for h in range(ROUTE_HEADS):
        off = 0
        for a in range(k):
            nb = k // (a + 1)
            cand_scr[h, off:off + nb, :] = sv_scr[2 * h, a:a + 1, :] + sv_scr[2 * h + 1, 0:nb, :]
            cid_scr[h, off:off + nb, :] = si_scr[2 * h, a:a + 1, :] * float(n) + si_scr[2 * h + 1, 0:nb, :]
            off += nb
    cs = [cand_scr[h] for h in range(ROUTE_HEADS)]
    ids = [cid_scr[h] for h in range(ROUTE_HEADS)]
    iota_c = lax.broadcasted_iota(jnp.int32, (ncand, tt), 0).astype(F32)
    for i in range(k):
        for h in range(ROUTE_HEADS):
            m = jnp.max(cs[h], axis=0, keepdims=True)
            tagged = jnp.where(cs[h] == m, iota_c, float(ncand))
            sel = tagged == jnp.min(tagged, axis=0, keepdims=True)
            r = h * k + i
            eid_ref[r:r + 1, :] = jnp.max(jnp.where(sel, ids[h], -1.0), axis=0, keepdims=True).astype(jnp.int32)
            ts_scr[r:r + 1, :] = m
            cs[h] = jnp.where(sel, -jnp.inf, cs[h])
    for h in range(ROUTE_HEADS):
        ts = ts_scr[h * k:(h + 1) * k, :]
        ex = jnp.exp(ts - ts[0:1, :])
        gate_ref[h * k:(h + 1) * k, :] = ex / jnp.sum(ex, axis=0, keepdims=True)


def _route(qp, keys2d):
    n_tok = qp.shape[0]
    tt = ROUTE_TT
    hp = ROUTE_HEADS
    k = PEER_TOPK
    npairs = sum(k // (a + 1) for a in range(k))
    ncand = -(-npairs // SUBLANES) * SUBLANES
    return pl.pallas_call(
        _route_kernel,
        grid=(n_tok // tt, PEER_HEADS // hp),
        in_specs=[
            pl.BlockSpec((tt, hp * 2 * PEER_HALF), lambda i, h: (i, h)),
            pl.BlockSpec((hp * 2 * PEER_NKEYS, PEER_HALF), lambda i, h: (h, 0)),
        ],
        out_specs=[pl.BlockSpec((hp * k, tt), lambda i, h: (h, i)),
                   pl.BlockSpec((hp * k, tt), lambda i, h: (h, i))],
        out_shape=[jax.ShapeDtypeStruct((PEER_SEL, n_tok), jnp.int32),
                   jax.ShapeDtypeStruct((PEER_SEL, n_tok), F32)],
        scratch_shapes=[pltpu.VMEM((2 * hp, k, tt), F32), pltpu.VMEM((2 * hp, k, tt), F32),
                        pltpu.VMEM((hp, ncand, tt), F32), pltpu.VMEM((hp, ncand, tt), F32),
                        pltpu.VMEM((hp * k, tt), F32)],
        compiler_params=_cparams(("parallel", "arbitrary")),
    )(qp, keys2d)


PEER_GROUP = SUBLANES
PEER_NGROUP = 4
PEER_TT = PEER_GROUP * PEER_NGROUP
U_MASK = 0xFFFF0000


PEER_CHUNKS = D_MODEL // LANES
PEER_SEL_TILES = PEER_SEL // SUBLANES


def _pack_tables(u, v):
    ub = lax.bitcast_convert_type(u.astype(BF16), jnp.uint16).astype(jnp.uint32)
    vb = lax.bitcast_convert_type(v.astype(BF16), jnp.uint16).astype(jnp.uint32)
    return ((ub << 16) | vb).reshape(u.shape[0], PEER_CHUNKS, LANES)


def _peer_kernel(eid_ref, eidn_ref, gate_ref, h2_ref, x1_ref, mod_ref, gf_ref, tab_hbm, o_ref, wbuf, sem, w_scr):
    i = pl.program_id(0)
    n = pl.num_programs(0)
    slot = i % 2
    G = PEER_GROUP

    def issue_rows(idx_ref, g, s, tok, j0, j1):
        for j in range(j0, j1):
            e = idx_ref[tok, j]
            pltpu.make_async_copy(tab_hbm.at[e], wbuf.at[s, tok, j], sem.at[s, g]).start(priority=j % 2)

    def wait_group(g, s):
        rows = pl.ds(g * G, G)
        pltpu.make_async_copy(wbuf.at[s, rows], wbuf.at[s, rows], sem.at[s, g]).wait()

    per_step = PEER_SEL // (2 * PEER_SEL_TILES)
    sub8 = lax.broadcasted_iota(jnp.int32, (SUBLANES, LANES), 0)

    def fold(x, y, h):
        low = (sub8 % (2 * h)) < h
        if 2 * h == SUBLANES:
            return jnp.where(low, x, y) + pltpu.roll(jnp.where(low, y, x), h, 0)
        return jnp.where(low, x + pltpu.roll(x, SUBLANES - h, 0), y + pltpu.roll(y, h, 0))

    fold_order = (0, 4, 2, 6, 1, 5, 3, 7)

    def chunk_sums(tiles):
        t = [tiles[k] for k in fold_order]
        t = [fold(t[0], t[1], 4), fold(t[2], t[3], 4), fold(t[4], t[5], 4), fold(t[6], t[7], 4)]
        t = [fold(t[0], t[1], 2), fold(t[2], t[3], 2)]
        return fold(t[0], t[1], 1)

    def compute_group(g, s):
        rows = pl.ds(pl.multiple_of(g * G, G), G)
        gate_t = gate_ref[rows, :].T
        h2g = h2_ref[rows, :]
        sub = lax.broadcasted_iota(jnp.int32, (G, D_MODEL), 0)
        y = jnp.zeros((G, D_MODEL), F32)
        for r in range(G):
            tok = g * G + r
            h2t = jnp.zeros((SUBLANES, LANES), F32)
            for c in range(PEER_CHUNKS):
                h2t = jnp.where(sub8 == c, h2g[r:r + 1, c * LANES:(c + 1) * LANES], h2t)
            part = []
            for a in range(PEER_SEL_TILES):
                issue_rows(eidn_ref, g, 1 - s, tok, a * per_step, (a + 1) * per_step)
                part.append(chunk_sums([
                    lax.bitcast_convert_type(wbuf[s, tok, a * SUBLANES + b] & jnp.uint32(U_MASK), F32) * h2t
                    for b in range(SUBLANES)]))
            act = jnp.sum(jnp.stack(part), axis=-1, keepdims=True)
            act = 0.5 * act * (1.0 + lax.erf(act * (1.0 / math.sqrt(2.0))))
            w = gate_t[:, r:r + 1].reshape(PEER_SEL_TILES, SUBLANES, 1) * act
            w_scr[...] = jnp.broadcast_to(w, (PEER_SEL_TILES, SUBLANES, LANES))
            acc = jnp.zeros((SUBLANES, LANES), F32)
            for a in range(PEER_SEL_TILES):
                issue_rows(eidn_ref, g, 1 - s, tok, (PEER_SEL_TILES + a) * per_step,
                           (PEER_SEL_TILES + a + 1) * per_step)
                for b in range(SUBLANES):
                    vf = lax.bitcast_convert_type(wbuf[s, tok, a * SUBLANES + b] << 16, F32)
                    acc = acc + w_scr[a, b:b + 1, :] * vf
            yrow = jnp.concatenate([acc[c:c + 1, :] for c in range(PEER_CHUNKS)], axis=-1)
            y = jnp.where(sub == r, yrow, y)
        x2 = x1_ref[rows, :] + mod_ref[5:6, :] * y
        o_ref[rows, :] = x2 * lax.rsqrt(jnp.mean(x2 * x2, axis=-1, keepdims=True) + EPS) * gf_ref[...]

    @pl.when(i == 0)
    def _():
        def first(t, carry):
            issue_rows(eid_ref, t // G, 0, t, 0, PEER_SEL)
            return carry
        lax.fori_loop(0, PEER_TT, first, 0)

    def step(g, carry):
        wait_group(g, slot)
        compute_group(g, slot)
        return carry

    lax.fori_loop(0, PEER_NGROUP, step, 0)

    @pl.when(i == n - 1)
    def _():
        def drain(g, carry):
            wait_group(g, 1 - slot)
            return carry
        lax.fori_loop(0, PEER_NGROUP, drain, 0)


def _peer(eid_t, gate_t, h2, x1, mod3, g_final, table, seq):
    n_tok = h2.shape[0]
    tt = PEER_TT
    n = n_tok // tt
    smem_idx = lambda f: pl.BlockSpec((tt, PEER_SEL), f, memory_space=pltpu.SMEM)
    tok = lambda w: pl.BlockSpec((tt, w), lambda i: (i, 0))
    return pl.pallas_call(
        _peer_kernel,
        grid=(n,),
        in_specs=[
            smem_idx(lambda i: (i, 0)),
            smem_idx(lambda i: (jnp.minimum(i + 1, n - 1), 0)),
            tok(PEER_SEL), tok(D_MODEL), tok(D_MODEL),
            pl.BlockSpec((None, 6, D_MODEL), lambda i: ((i * tt) // seq, 0, 0)),
            pl.BlockSpec((1, D_MODEL), lambda i: (0, 0)),
            pl.BlockSpec(memory_space=pl.ANY),
        ],
        out_specs=tok(D_MODEL),
        out_shape=jax.ShapeDtypeStruct((n_tok, D_MODEL), F32),
        scratch_shapes=[pltpu.VMEM((2, tt, PEER_SEL, PEER_CHUNKS, LANES), jnp.uint32),
                        pltpu.SemaphoreType.DMA((2, PEER_NGROUP)),
                        pltpu.VMEM((PEER_SEL_TILES, SUBLANES, LANES), F32)],
        compiler_params=_cparams(("arbitrary",)),
    )(eid_t, eid_t, gate_t, h2, x1, mod3, g_final, table)


def _run_group(x, c, w_mod, b_mod, g1, w_in_bf, sink, p_lb_fwd, p_lb_bwd, g_hnorm, woa, woh, wout, g2,
               wpq, keys2d, table, g_final):
    B, S, _ = x.shape
    mod3 = _modulation(c, w_mod, b_mod).reshape(B, 6, D_MODEL)
    proj = _inproj(x, mod3, g1, w_in_bf)
    att = _attention(proj, sink)
    hg = _hgrn(proj, p_lb_fwd, p_lb_bwd, g_hnorm)
    x1, h2, qp = _outproj(att, hg, proj, x, mod3, woa, woh, wout, g2, wpq)
    n_tok = B * S
    eid, gate = _route(qp.reshape(n_tok, -1), keys2d)
    y = _peer(eid.T, gate.T, h2.reshape(n_tok, D_MODEL), x1.reshape(n_tok, D_MODEL), mod3, g_final, table, S)
    return y.reshape(B, S, D_MODEL)


def kernel(x_prompt, x_sample, c_prompt, c_sample, w_mod, b_mod, g_norm1, w_in, att_sink, p_lb_fwd, p_lb_bwd,
           g_hnorm, w_o_att, w_o_hgrn, w_out, g_norm2, w_pq, peer_keys, peer_u, peer_v, g_final):
    assert w_mod.shape[0] == 1 and p_lb_fwd.shape[0] == 2, "single-layer encoder"
    shared = (
        w_mod[0], b_mod[0], g_norm1[0].reshape(1, D_MODEL), w_in[0].astype(BF16), att_sink[0],
        p_lb_fwd, p_lb_bwd, g_hnorm[0].reshape(1, HG_V),
        w_o_att[0].astype(BF16), w_o_hgrn[0].astype(BF16), w_out[0].astype(BF16),
        g_norm2[0].reshape(1, D_MODEL), w_pq[0].astype(BF16),
        peer_keys[0].reshape(PEER_HEADS * 2 * PEER_NKEYS, PEER_HALF),
        _pack_tables(peer_u[0], peer_v[0]), g_final.reshape(1, D_MODEL),
    )
    return (_run_group(x_prompt, c_prompt, *shared), _run_group(x_sample, c_sample, *shared))
```

```python
import functools
import math

import jax
import jax.numpy as jnp
from jax import lax
from jax.experimental import pallas as pl
from jax.experimental.pallas import tpu as pltpu

F32 = jnp.float32
BF16 = jnp.bfloat16
HIGHEST = lax.Precision.HIGHEST

D_MODEL = 1024
ATT_HEADS = 16
ATT_KV_HEADS = 4
ATT_HEAD_DIM = 64
ATT_GROUP = ATT_HEADS // ATT_KV_HEADS
WINDOW = 128
ATT_Q = ATT_HEADS * ATT_HEAD_DIM
ATT_KV = ATT_KV_HEADS * ATT_HEAD_DIM
HGRN_HEADS = 8
HGRN_DK = 128
HG_K = HGRN_HEADS * HGRN_DK
HG_V = HG_K
IN_COLS = ATT_Q + 2 * ATT_KV + 3 * HG_K + 2 * HG_V + 2 * D_MODEL
PEER_HEADS = 8
PEER_NKEYS = 128
PEER_HALF = 128
PEER_TOPK = 16
PEER_SEL = PEER_HEADS * PEER_TOPK
EPS = 1e-6

COL_QA = 0
COL_KA = COL_QA + ATT_Q
COL_VA = COL_KA + ATT_KV
COL_QH = COL_VA + ATT_KV
COL_ZF = COL_QH + HG_K
COL_ZB = COL_ZF + HG_K
COL_IH = COL_ZB + HG_K
COL_GH = COL_IH + HG_V
COL_GA = COL_GH + HG_V
COL_GB = COL_GA + D_MODEL

LANES = 128
SUBLANES = 8
VMEM_LIMIT = 56 * 1024 * 1024

HG_CHUNK = 128
HG_LEVELS = (64, 32, 16, 8, 4, 2, 1)


def _cparams(sem):
    return pltpu.CompilerParams(dimension_semantics=sem, vmem_limit_bytes=VMEM_LIMIT)


def _nt_dot(a, b):
    return lax.dot_general(a, b, (((1,), (1,)), ((), ())), preferred_element_type=F32)


def _tn_dot(a, b):
    return lax.dot_general(a, b, (((0,), (0,)), ((), ())), preferred_element_type=F32)


def _mod_kernel(c_ref, w_ref, b_ref, o_ref):
    c = c_ref[...]
    sc = c * jax.nn.sigmoid(c)
    o_ref[...] = jnp.dot(sc, w_ref[...], preferred_element_type=F32, precision=HIGHEST) + b_ref[...]


def _modulation(c, w_mod, b_mod):
    B = c.shape[0]
    n = w_mod.shape[1]
    tn = D_MODEL
    return pl.pallas_call(
        _mod_kernel,
        grid=(n // tn,),
        in_specs=[
            pl.BlockSpec((B, D_MODEL), lambda j: (0, 0)),
            pl.BlockSpec((D_MODEL, tn), lambda j: (0, j)),
            pl.BlockSpec((1, tn), lambda j: (0, j)),
        ],
        out_specs=pl.BlockSpec((B, tn), lambda j: (0, j)),
        out_shape=jax.ShapeDtypeStruct((B, n), F32),
        compiler_params=_cparams(("parallel",)),
    )(c, w_mod, b_mod.reshape(1, n))


def _inproj_kernel(x_ref, mod_ref, g_ref, w_ref, o_ref, h_scr):
    @pl.when(pl.program_id(2) == 0)
    def _():
        x = x_ref[...]
        y = x * lax.rsqrt(jnp.mean(x * x, axis=-1, keepdims=True) + EPS) * g_ref[...]
        h = y * (1.0 + mod_ref[1:2, :]) + mod_ref[0:1, :]
        h_scr[...] = h.astype(BF16)

    o_ref[...] = jnp.dot(h_scr[...], w_ref[...], preferred_element_type=F32)


def _inproj(x, mod3, g1, w_in_bf):
    B, S, _ = x.shape
    tm = min(S, 1024)
    tn = 2176
    return pl.pallas_call(
        _inproj_kernel,
        grid=(B, S // tm, IN_COLS // tn),
        in_specs=[
            pl.BlockSpec((None, tm, D_MODEL), lambda b, i, j: (b, i, 0)),
            pl.BlockSpec((None, 6, D_MODEL), lambda b, i, j: (b, 0, 0)),
            pl.BlockSpec((1, D_MODEL), lambda b, i, j: (0, 0)),
            pl.BlockSpec((D_MODEL, tn), lambda b, i, j: (0, j)),
        ],
        out_specs=pl.BlockSpec((None, tm, tn), lambda b, i, j: (b, i, j)),
        out_shape=jax.ShapeDtypeStruct((B, S, IN_COLS), F32),
        scratch_shapes=[pltpu.VMEM((tm, D_MODEL), BF16)],
        compiler_params=_cparams(("parallel", "parallel", "arbitrary")),
    )(x, mod3, g1, w_in_bf)


def _attn_kernel(sink_ref, q_ref, kp_ref, kc_ref, kn_ref, vp_ref, vc_ref, vn_ref, o_ref, s_scr, p_scr, r_scr,
                 *, nblk):
    j = pl.program_id(1)
    blk = WINDOW
    span = 3 * blk
    scale = 1.0 / math.sqrt(ATT_HEAD_DIM)
    q = (q_ref[...] * scale).astype(BF16)
    k = jnp.concatenate([kp_ref[...], kc_ref[...], kn_ref[...]], axis=0).astype(BF16)
    v = jnp.concatenate([vp_ref[...], vc_ref[...], vn_ref[...]], axis=0).astype(BF16)
    qi = lax.broadcasted_iota(jnp.int32, (blk, span), 0)
    kj = lax.broadcasted_iota(jnp.int32, (blk, span), 1)
    dist = jnp.abs(kj - blk - qi)
    lo = jnp.where(j == 0, blk, 0)
    hi = jnp.where(j == nblk - 1, 2 * blk, span)
    valid = (dist <= WINDOW) & (kj >= lo) & (kj < hi)
    neg_dist = jnp.where(valid, -dist.astype(F32), -jnp.inf)
    for hh in range(ATT_HEADS):
        h = hh // ATT_GROUP
        slope = 2.0 ** (-8.0 * (hh + 1) / ATT_HEADS)
        qh = q[:, hh * ATT_HEAD_DIM:(hh + 1) * ATT_HEAD_DIM]
        kh = k[:, h * ATT_HEAD_DIM:(h + 1) * ATT_HEAD_DIM]
        s_scr[hh] = _nt_dot(qh, kh) + slope * neg_dist
    for hh in range(ATT_HEADS):
        s = s_scr[hh]
        sink = sink_ref[hh]
        m = jnp.maximum(jnp.max(s, axis=-1, keepdims=True), sink)
        p = jnp.exp(s - m)
        r_scr[hh] = 1.0 / (jnp.sum(p, axis=-1, keepdims=True) + jnp.exp(sink - m))
        p_scr[hh] = p.astype(BF16)
    outs = []
    for hh in range(ATT_HEADS):
        h = hh // ATT_GROUP
        vh = v[:, h * ATT_HEAD_DIM:(h + 1) * ATT_HEAD_DIM]
        outs.append(jnp.dot(p_scr[hh], vh, preferred_element_type=F32) * r_scr[hh])
    o_ref[...] = jnp.concatenate(outs, axis=-1).astype(o_ref.dtype)


def _attention(proj, sink):
    B, S, _ = proj.shape
    blk = WINDOW
    nblk = S // blk
    kcol = COL_KA // ATT_KV
    vcol = COL_VA // ATT_KV

    def kv_spec(col, off):
        return pl.BlockSpec(
            (None, blk, ATT_KV),
            lambda b, j: (b, jnp.clip(j + off, 0, nblk - 1), col))

    return pl.pallas_call(
        functools.partial(_attn_kernel, nblk=nblk),
        grid=(B, nblk),
        in_specs=[
            pl.BlockSpec(memory_space=pltpu.SMEM),
            pl.BlockSpec((None, blk, ATT_Q), lambda b, j: (b, j, 0)),
            kv_spec(kcol, -1), kv_spec(kcol, 0), kv_spec(kcol, 1),
            kv_spec(vcol, -1), kv_spec(vcol, 0), kv_spec(vcol, 1),
        ],
        out_specs=pl.BlockSpec((None, blk, ATT_Q), lambda b, j: (b, j, 0)),
        out_shape=jax.ShapeDtypeStruct((B, S, ATT_Q), BF16),
        scratch_shapes=[pltpu.VMEM((ATT_HEADS, blk, 3 * blk), F32),
                        pltpu.VMEM((ATT_HEADS, blk, 3 * blk), BF16),
                        pltpu.VMEM((ATT_HEADS, blk, 1), F32)],
        compiler_params=_cparams(("parallel", "parallel")),
    )(sink, proj, proj, proj, proj, proj, proj, proj)


def _hgrn_prep(q_raw, z, lb, reverse):
    C = HG_CHUNK
    q = q_raw * jax.nn.sigmoid(q_raw)
    f = lb + (1.0 - lb) * jax.nn.sigmoid(z)
    kk = (1.0 - lb) * jax.nn.sigmoid(-z)
    logf = jnp.log(f) * (1.0 / math.log(2.0))
    row = lax.broadcasted_iota(jnp.int32, (C, C), 0)
    col = lax.broadcasted_iota(jnp.int32, (C, C), 1)
    tri = jnp.where((row <= col) if reverse else (row >= col), 1.0, 0.0).astype(F32)
    return q, kk, jnp.dot(tri, logf, preferred_element_type=F32, precision=HIGHEST)


def _hgrn_apply(q, kk, b, a, v, st_scr, reverse):
    C = HG_CHUNK
    edge = b[0:1, :] if reverse else b[C - 1:C, :]
    vb = v.astype(BF16)
    st = st_scr[...]
    o = _nt_dot((q * jnp.exp2(b)).astype(BF16), st.astype(BF16))
    kh = (kk * jnp.exp2(edge - b)).astype(BF16)
    st_scr[...] = st * jnp.exp2(edge) + _tn_dot(vb, kh)
    return o + jnp.dot(a, vb, preferred_element_type=F32)


def _hgrn_intra(q, kk, b, reverse):
    C = HG_CHUNK
    row = lax.broadcasted_iota(jnp.int32, (C, C), 0)
    col = lax.broadcasted_iota(jnp.int32, (C, C), 1)
    pos = lax.broadcasted_iota(jnp.int32, (C, 1), 0)
    shape3 = (C // SUBLANES, SUBLANES, HGRN_DK)
    b3 = b.reshape(shape3)
    sub = lax.broadcasted_iota(jnp.int32, shape3, 1)
    a = jnp.zeros((C, C), F32)
    for hsz in HG_LEVELS:
        seg = 2 * hsz
        if seg > SUBLANES:
            refs = []
            for s0 in range(0, C, seg):
                r = s0 + hsz if reverse else s0 + hsz - 1
                refs.append(jnp.broadcast_to(b[r:r + 1, :], (seg, HGRN_DK)))
            bmid = jnp.concatenate(refs, axis=0) if len(refs) > 1 else refs[0]
        else:
            bmid3 = None
            for s0 in reversed(range(0, SUBLANES, seg)):
                r = s0 + hsz if reverse else s0 + hsz - 1
                ref = jnp.broadcast_to(b3[:, r:r + 1, :], shape3)
                bmid3 = ref if bmid3 is None else jnp.where(sub < s0 + seg, ref, bmid3)
            bmid = bmid3.reshape(C, HGRN_DK)
        e = jnp.exp2(-jnp.abs(b - bmid))
        upper = (pos % seg) >= hsz
        qrole = jnp.logical_not(upper) if reverse else upper
        ql = jnp.where(qrole, q * e, 0.0).astype(BF16)
        kl = jnp.where(qrole, 0.0, kk * e).astype(BF16)
        al = _nt_dot(ql, kl)
        same = (row // seg) == (col // seg)
        rq = (row % seg) >= hsz
        ck = (col % seg) >= hsz
        if reverse:
            m = same & jnp.logical_not(rq) & ck
        else:
            m = same & rq & jnp.logical_not(ck)
        a = jnp.where(m, al, a)
    return jnp.where(row == col, jnp.sum(q * kk, axis=-1, keepdims=True), a).astype(BF16)


def _lower_bound(plb_ref):
    p = plb_ref[...]
    e = jnp.exp(p - jnp.max(p, axis=0, keepdims=True))
    return e[0:1, :] / jnp.sum(e, axis=0, keepdims=True)


def _hgrn_stage(q_ref, z_ref, lb, scr, nchunk, reverse):
    q_scr, k_scr, b_scr, a_scr = scr
    C = HG_CHUNK
    for ci in range(nchunk):
        sl = slice(ci * C, (ci + 1) * C)
        q_scr[sl, :], k_scr[sl, :], b_scr[sl, :] = _hgrn_prep(q_ref[sl, :], z_ref[sl, :], lb, reverse)
    for ci in range(nchunk):
        sl = slice(ci * C, (ci + 1) * C)
        a_scr[ci] = _hgrn_intra(q_scr[sl, :], k_scr[sl, :], b_scr[sl, :], reverse)


def _hgrn_fwd_kernel(q_ref, z_ref, v_ref, plb_ref, o_ref, st_scr, *scr, nchunk):
    @pl.when(pl.program_id(2) == 0)
    def _():
        st_scr[...] = jnp.zeros_like(st_scr)

    _hgrn_stage(q_ref, z_ref, _lower_bound(plb_ref), scr, nchunk, False)
    q_scr, k_scr, b_scr, a_scr = scr
    C = HG_CHUNK
    for ci in range(nchunk):
        sl = slice(ci * C, (ci + 1) * C)
        o_ref[sl, :] = _hgrn_apply(q_scr[sl, :], k_scr[sl, :], b_scr[sl, :], a_scr[ci], v_ref[sl, :], st_scr,
                                   False)


def _hgrn_bwd_kernel(q_ref, z_ref, v_ref, plb_ref, of_ref, gh_ref, gn_ref, o_ref, st_scr, *scr, nchunk):
    @pl.when(pl.program_id(2) == 0)
    def _():
        st_scr[...] = jnp.zeros_like(st_scr)

    _hgrn_stage(q_ref, z_ref, _lower_bound(plb_ref), scr, nchunk, True)
    q_scr, k_scr, b_scr, a_scr = scr
    C = HG_CHUNK
    for ci in reversed(range(nchunk)):
        sl = slice(ci * C, (ci + 1) * C)
        o = _hgrn_apply(q_scr[sl, :], k_scr[sl, :], b_scr[sl, :], a_scr[ci], v_ref[sl, :], st_scr, True)
        o = o + of_ref[sl, :]
        o = o * lax.rsqrt(jnp.mean(o * o, axis=-1, keepdims=True) + EPS) * gn_ref[...]
        gh = gh_ref[sl, :]
        o_ref[sl, :] = (o * (gh * jax.nn.sigmoid(gh))).astype(o_ref.dtype)


def _hgrn(proj, p_lb_fwd, p_lb_bwd, g_hnorm):
    B, S, _ = proj.shape
    tb = min(S, 2048)
    nt = S // tb
    nchunk = tb // HG_CHUNK
    hp = 1
    dk = HGRN_DK
    wd = dk
    scratch = ([pltpu.VMEM((dk, dk), F32)] + [pltpu.VMEM((tb, dk), F32) for _ in range(3)]
               + [pltpu.VMEM((nchunk, HG_CHUNK, HG_CHUNK), BF16)])

    def col_spec(col0, rev):
        if rev:
            return pl.BlockSpec((None, tb, wd), lambda b, h, t: (b, nt - 1 - t, col0 // wd + h))
        return pl.BlockSpec((None, tb, wd), lambda b, h, t: (b, t, col0 // wd + h))

    plb_spec = pl.BlockSpec((p_lb_fwd.shape[0], wd), lambda b, h, t: (0, h))
    sem = ("parallel", "parallel", "arbitrary")
    o_f = pl.pallas_call(
        functools.partial(_hgrn_fwd_kernel, nchunk=nchunk),
        grid=(B, HGRN_HEADS // hp, nt),
        in_specs=[col_spec(COL_QH, False), col_spec(COL_ZF, False), col_spec(COL_IH, False), plb_spec],
        out_specs=pl.BlockSpec((None, tb, wd), lambda b, h, t: (b, t, h)),
        out_shape=jax.ShapeDtypeStruct((B, S, HG_V), F32),
        scratch_shapes=scratch,
        compiler_params=_cparams(sem),
    )(proj, proj, proj, p_lb_fwd)
    return pl.pallas_call(
        functools.partial(_hgrn_bwd_kernel, nchunk=nchunk),
        grid=(B, HGRN_HEADS // hp, nt),
        in_specs=[col_spec(COL_QH, True), col_spec(COL_ZB, True), col_spec(COL_IH, True), plb_spec,
                  pl.BlockSpec((None, tb, wd), lambda b, h, t: (b, nt - 1 - t, h)),
                  col_spec(COL_GH, True),
                  pl.BlockSpec((1, wd), lambda b, h, t: (0, h))],
        out_specs=pl.BlockSpec((None, tb, wd), lambda b, h, t: (b, nt - 1 - t, h)),
        out_shape=jax.ShapeDtypeStruct((B, S, HG_V), BF16),
        scratch_shapes=scratch,
        compiler_params=_cparams(sem),
    )(proj, proj, proj, p_lb_bwd, o_f, proj, g_hnorm)


OUTPROJ_ROWS = 128


def _outproj_kernel(att_ref, hg_ref, ga0_ref, ga1_ref, gb0_ref, gb1_ref, x_ref, mod_ref, woa_ref, woh_ref,
                    wout_ref, g2_ref, wpq_ref, x1_ref, h2_ref, qp_ref, merged_scr, h2b_scr):
    tm = x_ref.shape[0]
    subs = [slice(r0, r0 + OUTPROJ_ROWS) for r0 in range(0, tm, OUTPROJ_ROWS)]
    for rs in subs:
        ya = jnp.dot(att_ref[rs, :], woa_ref[...], preferred_element_type=F32)
        yh = jnp.dot(hg_ref[rs, :], woh_ref[...], preferred_element_type=F32)
        ga = jnp.concatenate([ga0_ref[rs, :], ga1_ref[rs, :]], axis=-1)
        gb = jnp.concatenate([gb0_ref[rs, :], gb1_ref[rs, :]], axis=-1)
        merged_scr[rs, :] = (jax.nn.sigmoid(ga) * ya + jax.nn.sigmoid(gb) * yh).astype(BF16)
    for rs in subs:
        x1 = x_ref[rs, :] + mod_ref[2:3, :] * jnp.dot(merged_scr[rs, :], wout_ref[...],
                                                     preferred_element_type=F32)
        x1_ref[rs, :] = x1
        y = x1 * lax.rsqrt(jnp.mean(x1 * x1, axis=-1, keepdims=True) + EPS) * g2_ref[...]
        h2 = y * (1.0 + mod_ref[4:5, :]) + mod_ref[3:4, :]
        h2_ref[rs, :] = h2
        h2b_scr[rs, :] = h2.astype(BF16)
    for rs in subs:
        qp_ref[rs, :] = jnp.dot(h2b_scr[rs, :], wpq_ref[...], preferred_element_type=F32)


def _outproj(att, hg, proj, x, mod3, woa, woh, wout, g2, wpq):
    B, S, _ = x.shape
    tm = 512
    half = D_MODEL // 2
    nq = wpq.shape[1]

    def tok(width, col):
        return pl.BlockSpec((None, tm, width), lambda b, i: (b, i, col))

    def full(shape):
        return pl.BlockSpec(shape, lambda b, i: (0, 0))

    return pl.pallas_call(
        _outproj_kernel,
        grid=(B, S // tm),
        in_specs=[
            tok(ATT_Q, 0), tok(HG_V, 0),
            tok(half, COL_GA // half), tok(half, COL_GA // half + 1),
            tok(half, COL_GB // half), tok(half, COL_GB // half + 1),
            tok(D_MODEL, 0),
            pl.BlockSpec((None, 6, D_MODEL), lambda b, i: (b, 0, 0)),
            full((ATT_Q, D_MODEL)), full((HG_V, D_MODEL)), full((D_MODEL, D_MODEL)),
            full((1, D_MODEL)), full((D_MODEL, nq)),
        ],
        out_specs=[tok(D_MODEL, 0), tok(D_MODEL, 0), tok(nq, 0)],
        out_shape=[jax.ShapeDtypeStruct((B, S, D_MODEL), F32),
                   jax.ShapeDtypeStruct((B, S, D_MODEL), F32),
                   jax.ShapeDtypeStruct((B, S, nq), F32)],
        scratch_shapes=[pltpu.VMEM((tm, D_MODEL), BF16), pltpu.VMEM((tm, D_MODEL), BF16)],
        compiler_params=_cparams(("parallel", "parallel")),
    )(att, hg, proj, proj, proj, proj, x, mod3, woa, woh, wout, g2, wpq)


ROUTE_HEADS = 4
ROUTE_TT = 256


def _route_kernel(q_ref, keys_ref, eid_ref, gate_ref, sv_scr, si_scr, cand_scr, cid_scr, ts_scr):
    tt = q_ref.shape[0]
    n = PEER_NKEYS
    k = PEER_TOPK
    nch = 2 * ROUTE_HEADS
    q = q_ref[...].astype(BF16)
    iota_n = lax.broadcasted_iota(jnp.int32, (n, tt), 0).astype(F32)
    s = [_nt_dot(keys_ref[c * n:(c + 1) * n, :].astype(BF16), q[:, c * PEER_HALF:(c + 1) * PEER_HALF])
         for c in range(nch)]
    for i in range(k):
        for c in range(nch):
            m = jnp.max(s[c], axis=0, keepdims=True)
            tagged = jnp.where(s[c] == m, iota_n, float(n))
            idx = jnp.min(tagged, axis=0, keepdims=True)
            sv_scr[c, i:i + 1, :] = m
            si_scr[c, i:i + 1, :] = idx
            s[c] = jnp.where(tagged == idx, -jnp.inf, s[c])
    ncand = cand_scr.shape[1]
    cand_scr[...] = jnp.full(cand_scr.shape, -jnp.inf, F32)
    cid_scr[...] = jnp.zeros(cid_scr.shape, F32)
    for h in range(ROUTE_HEADS):
        off = 0
        for a in range(k):
            nb = k // (a + 1)
            cand_scr[h, off:off + nb, :] = sv_scr[2 * h, a:a + 1, :] + sv_scr[2 * h + 1, 0:nb, :]
            cid_scr[h, off:off + nb, :] = si_scr[2 * h, a:a + 1, :] * float(n) + si_scr[2 * h + 1, 0:nb, :]
            off += nb
    cs = [cand_scr[h] for h in range(ROUTE_HEADS)]
    ids = [cid_scr[h] for h in range(ROUTE_HEADS)]
    iota_c = lax.broadcasted_iota(jnp.int32, (ncand, tt), 0).astype(F32)
    for i in range(k):
        for h in range(ROUTE_HEADS):
            m = jnp.max(cs[h], axis=0, keepdims=True)
            tagged = jnp.where(cs[h] == m, iota_c, float(ncand))
            sel = tagged == jnp.min(tagged, axis=0, keepdims=True)
            r = h * k + i
            eid_ref[r:r + 1, :] = jnp.max(jnp.where(sel, ids[h], -1.0), axis=0, keepdims=True).astype(jnp.int32)
            ts_scr[r:r + 1, :] = m
            cs[h] = jnp.where(sel, -jnp.inf, cs[h])
    for h in range(ROUTE_HEADS):
        ts = ts_scr[h * k:(h + 1) * k, :]
        ex = jnp.exp(ts - ts[0:1, :])
        gate_ref[h * k:(h + 1) * k, :] = ex / jnp.sum(ex, axis=0, keepdims=True)


def _route(qp, keys2d):
    n_tok = qp.shape[0]
    tt = ROUTE_TT
    hp = ROUTE_HEADS
    k = PEER_TOPK
    npairs = sum(k // (a + 1) for a in range(k))
    ncand = -(-npairs // SUBLANES) * SUBLANES
    return pl.pallas_call(
        _route_kernel,
        grid=(n_tok // tt, PEER_HEADS // hp),
        in_specs=[
            pl.BlockSpec((tt, hp * 2 * PEER_HALF), lambda i, h: (i, h)),
            pl.BlockSpec((hp * 2 * PEER_NKEYS, PEER_HALF), lambda i, h: (h, 0)),
        ],
        out_specs=[pl.BlockSpec((hp * k, tt), lambda i, h: (h, i)),
                   pl.BlockSpec((hp * k, tt), lambda i, h: (h, i))],
        out_shape=[jax.ShapeDtypeStruct((PEER_SEL, n_tok), jnp.int32),
                   jax.ShapeDtypeStruct((PEER_SEL, n_tok), F32)],
        scratch_shapes=[pltpu.VMEM((2 * hp, k, tt), F32), pltpu.VMEM((2 * hp, k, tt), F32),
                        pltpu.VMEM((hp, ncand, tt), F32), pltpu.VMEM((hp, ncand, tt), F32),
                        pltpu.VMEM((hp * k, tt), F32)],
        compiler_params=_cparams(("parallel", "arbitrary")),
    )(qp, keys2d)


PEER_GROUP = SUBLANES
PEER_NGROUP = 4
PEER_TT = PEER_GROUP * PEER_NGROUP
U_MASK = 0xFFFF0000


PEER_CHUNKS = D_MODEL // LANES
PEER_SEL_TILES = PEER_SEL // SUBLANES


def _pack_tables(u, v):
    ub = lax.bitcast_convert_type(u.astype(BF16), jnp.uint16).astype(jnp.uint32)
    vb = lax.bitcast_convert_type(v.astype(BF16), jnp.uint16).astype(jnp.uint32)
    return ((ub << 16) | vb).reshape(u.shape[0], PEER_CHUNKS, LANES)


def _peer_kernel(eid_ref, eidn_ref, gate_ref, h2_ref, x1_ref, mod_ref, gf_ref, tab_hbm, o_ref, wbuf, sem, w_scr):
    i = pl.program_id(0)
    n = pl.num_programs(0)
    slot = i % 2
    G = PEER_GROUP

    def issue_rows(idx_ref, g, s, tok, j0, j1):
        for j in range(j0, j1):
            e = idx_ref[tok, j]
            pltpu.make_async_copy(tab_hbm.at[e], wbuf.at[s, tok, j], sem.at[s, g]).start(priority=j % 2)

    def wait_group(g, s):
        rows = pl.ds(g * G, G)
        pltpu.make_async_copy(wbuf.at[s, rows], wbuf.at[s, rows], sem.at[s, g]).wait()

    per_step = PEER_SEL // (2 * PEER_SEL_TILES)
    sub8 = lax.broadcasted_iota(jnp.int32, (SUBLANES, LANES), 0)

    def fold(x, y, h):
        low = (sub8 % (2 * h)) < h
        if 2 * h == SUBLANES:
            return jnp.where(low, x, y) + pltpu.roll(jnp.where(low, y, x), h, 0)
        return jnp.where(low, x + pltpu.roll(x, SUBLANES - h, 0), y + pltpu.roll(y, h, 0))

    fold_order = (0, 4, 2, 6, 1, 5, 3, 7)

    def chunk_sums(tiles):
        t = [tiles[k] for k in fold_order]
        t = [fold(t[0], t[1], 4), fold(t[2], t[3], 4), fold(t[4], t[5], 4), fold(t[6], t[7], 4)]
        t = [fold(t[0], t[1], 2), fold(t[2], t[3], 2)]
        return fold(t[0], t[1], 1)

    def compute_group(g, s):
        rows = pl.ds(pl.multiple_of(g * G, G), G)
        gate_t = gate_ref[rows, :].T
        h2g = h2_ref[rows, :]
        sub = lax.broadcasted_iota(jnp.int32, (G, D_MODEL), 0)
        y = jnp.zeros((G, D_MODEL), F32)
        for r in range(G):
            tok = g * G + r
            h2t = jnp.zeros((SUBLANES, LANES), F32)
            for c in range(PEER_CHUNKS):
                h2t = jnp.where(sub8 == c, h2g[r:r + 1, c * LANES:(c + 1) * LANES], h2t)
            part = []
            for a in range(PEER_SEL_TILES):
                issue_rows(eidn_ref, g, 1 - s, tok, a * per_step, (a + 1) * per_step)
                part.append(chunk_sums([
                    lax.bitcast_convert_type(wbuf[s, tok, a * SUBLANES + b] & jnp.uint32(U_MASK), F32) * h2t
                    for b in range(SUBLANES)]))
            act = jnp.sum(jnp.stack(part), axis=-1, keepdims=True)
            act = 0.5 * act * (1.0 + lax.erf(act * (1.0 / math.sqrt(2.0))))
            w = gate_t[:, r:r + 1].reshape(PEER_SEL_TILES, SUBLANES, 1) * act
            w_scr[...] = jnp.broadcast_to(w, (PEER_SEL_TILES, SUBLANES, LANES))
            acc = jnp.zeros((SUBLANES, LANES), F32)
            for a in range(PEER_SEL_TILES):
                issue_rows(eidn_ref, g, 1 - s, tok, (PEER_SEL_TILES + a) * per_step,
                           (PEER_SEL_TILES + a + 1) * per_step)
                for b in range(SUBLANES):
                    vf = lax.bitcast_convert_type(wbuf[s, tok, a * SUBLANES + b] << 16, F32)
                    acc = acc + w_scr[a, b:b + 1, :] * vf
            yrow = jnp.concatenate([acc[c:c + 1, :] for c in range(PEER_CHUNKS)], axis=-1)
            y = jnp.where(sub == r, yrow, y)
        x2 = x1_ref[rows, :] + mod_ref[5:6, :] * y
        o_ref[rows, :] = x2 * lax.rsqrt(jnp.mean(x2 * x2, axis=-1, keepdims=True) + EPS) * gf_ref[...]

    @pl.when(i == 0)
    def _():
        def first(t, carry):
            issue_rows(eid_ref, t // G, 0, t, 0, PEER_SEL)
            return carry
        lax.fori_loop(0, PEER_TT, first, 0)

    def step(g, carry):
        wait_group(g, slot)
        compute_group(g, slot)
        return carry

    lax.fori_loop(0, PEER_NGROUP, step, 0)

    @pl.when(i == n - 1)
    def _():
        def drain(g, carry):
            wait_group(g, 1 - slot)
            return carry
        lax.fori_loop(0, PEER_NGROUP, drain, 0)


def _peer(eid_t, gate_t, h2, x1, mod3, g_final, table, seq):
    n_tok = h2.shape[0]
    tt = PEER_TT
    n = n_tok // tt
    smem_idx = lambda f: pl.BlockSpec((tt, PEER_SEL), f, memory_space=pltpu.SMEM)
    tok = lambda w: pl.BlockSpec((tt, w), lambda i: (i, 0))
    return pl.pallas_call(
        _peer_kernel,
        grid=(n,),
        in_specs=[
            smem_idx(lambda i: (i, 0)),
            smem_idx(lambda i: (jnp.minimum(i + 1, n - 1), 0)),
            tok(PEER_SEL), tok(D_MODEL), tok(D_MODEL),
            pl.BlockSpec((None, 6, D_MODEL), lambda i: ((i * tt) // seq, 0, 0)),
            pl.BlockSpec((1, D_MODEL), lambda i: (0, 0)),
            pl.BlockSpec(memory_space=pl.ANY),
        ],
        out_specs=tok(D_MODEL),
        out_shape=jax.ShapeDtypeStruct((n_tok, D_MODEL), F32),
        scratch_shapes=[pltpu.VMEM((2, tt, PEER_SEL, PEER_CHUNKS, LANES), jnp.uint32),
                        pltpu.SemaphoreType.DMA((2, PEER_NGROUP)),
                        pltpu.VMEM((PEER_SEL_TILES, SUBLANES, LANES), F32)],
        compiler_params=_cparams(("arbitrary",)),
    )(eid_t, eid_t, gate_t, h2, x1, mod3, g_final, table)


def _run_group(x, c, w_mod, b_mod, g1, w_in_bf, sink, p_lb_fwd, p_lb_bwd, g_hnorm, woa, woh, wout, g2,
               wpq, keys2d, table, g_final):
    B, S, _ = x.shape
    mod3 = _modulation(c, w_mod, b_mod).reshape(B, 6, D_MODEL)
    proj = _inproj(x, mod3, g1, w_in_bf)
    att = _attention(proj, sink)
    hg = _hgrn(proj, p_lb_fwd, p_lb_bwd, g_hnorm)
    x1, h2, qp = _outproj(att, hg, proj, x, mod3, woa, woh, wout, g2, wpq)
    n_tok = B * S
    eid, gate = _route(qp.reshape(n_tok, -1), keys2d)
    y = _peer(eid.T, gate.T, h2.reshape(n_tok, D_MODEL), x1.reshape(n_tok, D_MODEL), mod3, g_final, table, S)
    return y.reshape(B, S, D_MODEL)


def kernel(x_prompt, x_sample, c_prompt, c_sample, w_mod, b_mod, g_norm1, w_in, att_sink, p_lb_fwd, p_lb_bwd,
           g_hnorm, w_o_att, w_o_hgrn, w_out, g_norm2, w_pq, peer_keys, peer_u, peer_v, g_final):
    assert w_mod.shape[0] == 1 and p_lb_fwd.shape[0] == 2, "single-layer encoder"
    shared = (
        w_mod[0], b_mod[0], g_norm1[0].reshape(1, D_MODEL), w_in[0].astype(BF16), att_sink[0],
        p_lb_fwd, p_lb_bwd, g_hnorm[0].reshape(1, HG_V),
        w_o_att[0].astype(BF16), w_o_hgrn[0].astype(BF16), w_out[0].astype(BF16),
        g_norm2[0].reshape(1, D_MODEL), w_pq[0].astype(BF16),
        peer_keys[0].reshape(PEER_HEADS * 2 * PEER_NKEYS, PEER_HALF),
        _pack_tables(peer_u[0], peer_v[0]), g_final.reshape(1, D_MODEL),
    )
    return (_run_group(x_prompt, c_prompt, *shared), _run_group(x_sample, c_sample, *shared))
```
